```python
import math
import jax, jax.numpy as jnp
from jax import lax
import numpy as np

D_MODEL = 1024
BATCH = 2
SEQ = 16384
DEPTH = 2

HEAD_DIM = 64
N_HEADS_A = 8
N_HEADS_B = 8
D_A = N_HEADS_A * HEAD_DIM
D_B = N_HEADS_B * HEAD_DIM
N_IDX_HEADS = 8
IDX_DIM = 64
D_RNN = 512
N_RNN_BLOCKS = 8
RNN_BLOCK = D_RNN // N_RNN_BLOCKS
CONV_WIDTH = 4
LRU_C = 8.0
D_FF = 4 * D_MODEL
MOBA_BLOCK = 256
MOBA_TOPK = 3
DSA_MAX_TOPK = 256
Q_BLOCK = 128
N_BUCKETS = 32
MAX_DISTANCE = 128
N_BRANCHES = 3
N_ATTN_HEADS = N_HEADS_A + N_HEADS_B
EPS = 1e-6
NEG = -1e30

IN_SPLITS = (D_A, D_A, D_A,
             D_B, D_B, D_B,
             N_IDX_HEADS * IDX_DIM,
             IDX_DIM,
             N_IDX_HEADS,
             D_RNN, D_RNN,
             N_BRANCHES * D_MODEL)
D_IN = sum(IN_SPLITS)
IN_OFFSETS = tuple(int(o) for o in np.cumsum(IN_SPLITS)[:-1])

kernel_name = "hybrid_moba_dsa_rglru_block"


def rms_norm(x, g):
    xf = x.astype(jnp.float32)
    y = xf * lax.rsqrt(jnp.mean(xf * xf, axis=-1, keepdims=True) + EPS)
    return (y * g.astype(jnp.float32)).astype(x.dtype)


def t5_bucket(dist):
    n = jnp.maximum(dist, 0)
    max_exact = N_BUCKETS // 2
    nf = jnp.maximum(n, 1).astype(jnp.float32)
    large = max_exact + (jnp.log(nf / max_exact) / math.log(MAX_DISTANCE / max_exact)
                         * (N_BUCKETS - max_exact)).astype(jnp.int32)
    large = jnp.minimum(large, N_BUCKETS - 1)
    return jnp.where(n < max_exact, n, large)


def moba_attention(q, k, v, bias_tab):
    B, S, H, dh = q.shape
    nb = -(-S // MOBA_BLOCK)
    pad = nb * MOBA_BLOCK - S
    kp = jnp.pad(k, ((0, 0), (0, pad), (0, 0), (0, 0)))
    vp = jnp.pad(v, ((0, 0), (0, pad), (0, 0), (0, 0)))
    kblk = kp.reshape(B, nb, MOBA_BLOCK, H, dh).transpose(0, 3, 1, 2, 4)
    vblk = vp.reshape(B, nb, MOBA_BLOCK, H, dh).transpose(0, 3, 1, 2, 4)
    kmean = jnp.mean(kblk, axis=3)
    n_sel = min(MOBA_TOPK, nb)
    scale = dh ** -0.5
    b_ix = jnp.arange(B)[:, None, None, None]
    h_ix = jnp.arange(H)[None, None, :, None]
    blk_ids = jnp.arange(nb)
    bias_t = bias_tab.T

    def one_block(qb):
        t0 = qb * Q_BLOCK
        cur = t0 // MOBA_BLOCK
        t = t0 + jnp.arange(Q_BLOCK)
        qq = lax.dynamic_slice_in_dim(q, t0, Q_BLOCK, axis=1)
        gate = jnp.einsum('bqhd,bhnd->bqhn', qq, kmean).astype(jnp.float32)
        gate = jnp.where(blk_ids < cur, gate, NEG)
        _, sel = lax.top_k(gate, n_sel)
        sel_ok = sel < cur
        kg = kblk[b_ix, h_ix, sel]
        vg = vblk[b_ix, h_ix, sel]
        s_past = jnp.einsum('bqhd,bqhnkd->bqhnk', qq, kg).astype(jnp.float32) * scale
        pos_past = sel[..., None] * MOBA_BLOCK + jnp.arange(MOBA_BLOCK)
        bias_past = bias_t[h_ix[..., None], t5_bucket(t[None, :, None, None, None] - pos_past)]
        s_past = jnp.where(sel_ok[..., None], s_past + bias_past, NEG)
        k_own = lax.dynamic_slice_in_dim(kp, cur * MOBA_BLOCK, MOBA_BLOCK, axis=1)
        v_own = lax.dynamic_slice_in_dim(vp, cur * MOBA_BLOCK, MOBA_BLOCK, axis=1)
        rel = t[:, None] - (cur * MOBA_BLOCK + jnp.arange(MOBA_BLOCK))[None, :]
        bias_own = bias_tab[t5_bucket(rel)].transpose(0, 2, 1)[None]
        s_own = jnp.einsum('bqhd,bkhd->bqhk', qq, k_own).astype(jnp.float32) * scale + bias_own
        s_own = jnp.where((rel >= 0)[None, :, None, :], s_own, NEG)
        logits = jnp.concatenate([s_past.reshape(B, Q_BLOCK, H, n_sel * MOBA_BLOCK), s_own], axis=-1)
        p = jax.nn.softmax(logits, axis=-1).astype(v.dtype)
        p_past = p[..., :n_sel * MOBA_BLOCK].reshape(B, Q_BLOCK, H, n_sel, MOBA_BLOCK)
        p_own = p[..., n_sel * MOBA_BLOCK:]
        return (jnp.einsum('bqhnk,bqhnkd->bqhd', p_past, vg)
                + jnp.einsum('bqhk,bkhd->bqhd', p_own, v_own))

    out = lax.map(one_block, jnp.arange(S // Q_BLOCK))
    return out.transpose(1, 0, 2, 3, 4).reshape(B, S, H * dh)


def dsa_attention(q, k, v, qi, ki, wi, bias_tab):
    B, S, H, dh = q.shape
    n_keep = min(DSA_MAX_TOPK, S // 4)
    scale = dh ** -0.5
    idx_scale = (N_IDX_HEADS ** -0.5) * (IDX_DIM ** -0.5)
    s_all = jnp.arange(S)

    def one_block(qb):
        t0 = qb * Q_BLOCK
        t = t0 + jnp.arange(Q_BLOCK)
        qq = lax.dynamic_slice_in_dim(q, t0, Q_BLOCK, axis=1)
        qiq = lax.dynamic_slice_in_dim(qi, t0, Q_BLOCK, axis=1)
        wq = lax.dynamic_slice_in_dim(wi, t0, Q_BLOCK, axis=1)
        dots = jnp.einsum('bqhd,bsd->bqhs', qiq, ki)
        score = jnp.einsum('bqh,bqhs->bqs', wq, jax.nn.relu(dots)).astype(jnp.float32) * idx_scale
        score = jnp.where((s_all[None, :] <= t[:, None])[None], score, NEG)
        _, idx = lax.top_k(score, n_keep)
        ok = idx <= t[None, :, None]
        kg = jax.vmap(lambda kk, ii: kk[ii])(k, idx)
        vg = jax.vmap(lambda vv, ii: vv[ii])(v, idx)
        bias = bias_tab[t5_bucket(t[None, :, None] - idx)].transpose(0, 1, 3, 2)
        logits = jnp.einsum('bqhd,bqkhd->bqhk', qq, kg).astype(jnp.float32) * scale + bias
        logits = jnp.where(ok[:, :, None, :], logits, NEG)
        p = jax.nn.softmax(logits, axis=-1).astype(v.dtype)
        return jnp.einsum('bqhk,bqkhd->bqhd', p, vg)

    out = lax.map(one_block, jnp.arange(S // Q_BLOCK))
    return out.transpose(1, 0, 2, 3, 4).reshape(B, S, H * dh)


def causal_depthwise_conv(x, w, b):
    S = x.shape[1]
    xp = jnp.pad(x, ((0, 0), (CONV_WIDTH - 1, 0), (0, 0)))
    y = b
    for i in range(CONV_WIDTH):
        y = y + w[i] * xp[:, i:i + S]
    return y


def rg_lru(x, w_r, b_r, w_i, b_i, lam):
    B, S, C = x.shape
    xb = x.reshape(B, S, N_RNN_BLOCKS, RNN_BLOCK)
    r = jax.nn.sigmoid(jnp.einsum('bsnc,ncd->bsnd', xb, w_r).reshape(B, S, C) + b_r)
    i = jax.nn.sigmoid(jnp.einsum('bsnc,ncd->bsnd', xb, w_i).reshape(B, S, C) + b_i)
    log_a = -LRU_C * r.astype(jnp.float32) * jax.nn.softplus(-lam.astype(jnp.float32))
    a = jnp.exp(log_a)
    u = jnp.sqrt(-jnp.expm1(2.0 * log_a)) * (i * x).astype(jnp.float32)

    def combine(left, right):
        a1, b1 = left
        a2, b2 = right
        return a1 * a2, a2 * b1 + b2

    _, h = lax.associative_scan(combine, (a, u), axis=1)
    return h.astype(x.dtype)


def token_mixer(xn, w_in, conv_w, conv_b, w_r, b_r, w_i, b_i, lam, w_pa, w_pb, w_pc, w_o, rel_bias):
    B, S, _ = xn.shape
    proj = xn @ w_in
    (qa, ka, va, qb, kb, vb, qi, ki, wi, xc, yc, gates) = jnp.split(proj, IN_OFFSETS, axis=-1)
    heads_a = lambda z: z.reshape(B, S, N_HEADS_A, HEAD_DIM)
    heads_b = lambda z: z.reshape(B, S, N_HEADS_B, HEAD_DIM)
    y_a = moba_attention(heads_a(qa), heads_a(ka), heads_a(va), rel_bias[:, :N_HEADS_A])
    y_b = dsa_attention(heads_b(qb), heads_b(kb), heads_b(vb),
                        qi.reshape(B, S, N_IDX_HEADS, IDX_DIM), ki, wi, rel_bias[:, N_HEADS_A:])
    y_c = rg_lru(causal_depthwise_conv(xc, conv_w, conv_b), w_r, b_r, w_i, b_i, lam) * jax.nn.gelu(yc)
    g = jax.nn.sigmoid(gates).reshape(B, S, N_BRANCHES, D_MODEL)
    merged = g[:, :, 0] * (y_a @ w_pa) + g[:, :, 1] * (y_b @ w_pb) + g[:, :, 2] * (y_c @ w_pc)
    return merged @ w_o


def squared_relu_mlp(xn, w_up, w_down):
    return jnp.square(jax.nn.relu(xn @ w_up)) @ w_down


def setup_inputs(seed: int = 0) -> dict:
    key = jax.random.key(seed)
    ks = jax.random.split(key, 20)
    f32 = jnp.float32
    nrm = lambda k, shape, s: jax.random.normal(k, shape, f32) * s
    u = jax.random.uniform(ks[10], (DEPTH, D_RNN), f32, minval=0.9, maxval=0.999)
    a = u ** (1.0 / LRU_C)
    lam = jnp.log(a) - jnp.log1p(-a)
    return {
        "x": nrm(ks[0], (BATCH, SEQ, D_MODEL), 1.0),
        "rel_bias": nrm(ks[1], (N_BUCKETS, N_ATTN_HEADS), 0.5),
        "norm_mix_g": 1.0 + nrm(ks[2], (DEPTH, D_MODEL), 0.05),
        "w_in": nrm(ks[3], (DEPTH, D_MODEL, D_IN), D_MODEL ** -0.5),
        "conv_w": nrm(ks[4], (DEPTH, CONV_WIDTH, D_RNN), CONV_WIDTH ** -0.5),
        "conv_b": nrm(ks[5], (DEPTH, D_RNN), 0.02),
        "w_r": nrm(ks[6], (DEPTH, N_RNN_BLOCKS, RNN_BLOCK, RNN_BLOCK), RNN_BLOCK ** -0.5),
        "b_r": nrm(ks[7], (DEPTH, D_RNN), 0.1),
        "w_i": nrm(ks[8], (DEPTH, N_RNN_BLOCKS, RNN_BLOCK, RNN_BLOCK), RNN_BLOCK ** -0.5),
        "b_i": nrm(ks[9], (DEPTH, D_RNN), 0.1),
        "lru_lambda": lam,
        "w_pa": nrm(ks[11], (DEPTH, D_A, D_MODEL), D_A ** -0.5),
        "w_pb": nrm(ks[12], (DEPTH, D_B, D_MODEL), D_B ** -0.5),
        "w_pc": nrm(ks[13], (DEPTH, D_RNN, D_MODEL), D_RNN ** -0.5),
        "w_o": nrm(ks[14], (DEPTH, D_MODEL, D_MODEL), D_MODEL ** -0.5),
        "norm_mlp_g": 1.0 + nrm(ks[15], (DEPTH, D_MODEL), 0.05),
        "w_up": nrm(ks[16], (DEPTH, D_MODEL, D_FF), D_MODEL ** -0.5),
        "w_down": nrm(ks[17], (DEPTH, D_FF, D_MODEL), D_FF ** -0.5),
        "final_norm_g": 1.0 + nrm(ks[18], (D_MODEL,), 0.05),
    }


def reference(x, rel_bias, norm_mix_g, w_in, conv_w, conv_b, w_r, b_r, w_i, b_i, lru_lambda,
              w_pa, w_pb, w_pc, w_o, norm_mlp_g, w_up, w_down, final_norm_g):
    for l in range(DEPTH):
        x = x + token_mixer(rms_norm(x, norm_mix_g[l]), w_in[l], conv_w[l], conv_b[l],
                            w_r[l], b_r[l], w_i[l], b_i[l], lru_lambda[l],
                            w_pa[l], w_pb[l], w_pc[l], w_o[l], rel_bias)
        x = x + squared_relu_mlp(rms_norm(x, norm_mlp_g[l]), w_up[l], w_down[l])
    return rms_norm(x, final_norm_g)
```

```python
import functools
import math

import numpy as np
import jax
import jax.numpy as jnp
from jax import lax
from jax.experimental import pallas as pl
from jax.experimental.pallas import tpu as pltpu

F32 = jnp.float32
MXU_DTYPE = jnp.bfloat16

HEAD_DIM = 64
N_HEADS = 8
D_ATT = N_HEADS * HEAD_DIM
IDX_DIM = 64
N_IDX_HEADS = 8
D_RNN = 512
CONV_WIDTH = 4
LRU_C = 8.0
MOBA_TOPK = 3
DSA_MAX_TOPK = 256
N_BUCKETS = 32
MAX_DISTANCE = 128
EPS = 1e-6
NEG = -1e30
M_INIT = -1e29
ATT_SCALE = HEAD_DIM ** -0.5
IDX_SCALE = (N_IDX_HEADS ** -0.5) * (IDX_DIM ** -0.5)

TILE = 256
MAX_BLOCKS = 64
KEY_CHUNK = 512
VMEM_LIMIT = 56 * 1024 * 1024
INT_MIN = -2 ** 31


def _cparams(sem):
    return pltpu.CompilerParams(dimension_semantics=sem, vmem_limit_bytes=VMEM_LIMIT)


def _resident(shape):
    nd = len(shape)
    return pl.BlockSpec(shape, lambda *_: (0,) * nd, pipeline_mode=pl.Buffered(1))


def _rms(x, g):
    return x * lax.rsqrt(jnp.mean(x * x, axis=-1, keepdims=True) + EPS) * g


def _dot(a, b):
    return jnp.dot(a, b, preferred_element_type=F32)


def _dot_nt(a, b):
    return lax.dot_general(a, b, (((1,), (1,)), ((), ())), preferred_element_type=F32)


N_ATT_COLS = 7 * D_ATT
SMALL_COLS = 128
W_CAT_COLS = N_ATT_COLS + SMALL_COLS + 2 * D_RNN + 3 * 1024


def _inproj_kernel(x_ref, g_ref, w_ref, qa, ka, va, qb, kb, vb, qi, ki, wi, xc, yc, gt, kmean):
    xn = _rms(x_ref[...], g_ref[...]).astype(MXU_DTYPE)

    def mm(lo, width):
        return _dot(xn, w_ref[:, lo:lo + width])

    for n, o in enumerate((qa, ka, va, qb, kb, vb, qi)):
        r = mm(n * D_ATT, D_ATT)
        if n in (0, 3):
            r = r * ATT_SCALE
        if n == 1:
            kmean[0] = jnp.mean(r, axis=0, keepdims=True)
        o[...] = r.astype(o.dtype)
    r = mm(N_ATT_COLS, SMALL_COLS)
    ki[...] = r[:, :IDX_DIM].astype(ki.dtype)
    wi[...] = r[:, IDX_DIM:IDX_DIM + N_IDX_HEADS]
    lo = N_ATT_COLS + SMALL_COLS
    xc[...] = mm(lo, D_RNN)
    yc[...] = mm(lo + D_RNN, D_RNN)
    lo += 2 * D_RNN
    for c in range(3):
        gt[:, c * 1024:(c + 1) * 1024] = mm(lo + c * 1024, 1024)


def _inproj(x2, g, w_cat):
    T, D = x2.shape
    nt = T // TILE
    row = lambda w: pl.BlockSpec((TILE, w), lambda t: (t, 0))
    att = jax.ShapeDtypeStruct((T, D_ATT), MXU_DTYPE)
    out_shape = [att] * 7 + [
        jax.ShapeDtypeStruct((T, IDX_DIM), MXU_DTYPE),
        jax.ShapeDtypeStruct((T, N_IDX_HEADS), F32),
        jax.ShapeDtypeStruct((T, D_RNN), F32),
        jax.ShapeDtypeStruct((T, D_RNN), F32),
        jax.ShapeDtypeStruct((T, 3 * D), F32),
        jax.ShapeDtypeStruct((nt, 1, D_ATT), F32),
    ]
    out_specs = [row(D_ATT)] * 7 + [row(IDX_DIM), row(N_IDX_HEADS), row(D_RNN), row(D_RNN), row(3 * D),
                                    pl.BlockSpec((1, 1, D_ATT), lambda t: (t, 0, 0))]
    return pl.pallas_call(
        _inproj_kernel,
        grid=(nt,),
        in_specs=[row(D), _resident((1, D)), _resident(w_cat.shape)],
        out_specs=out_specs,
        out_shape=out_shape,
        compiler_params=_cparams(("arbitrary",)),
        name="inproj",
    )(x2, g, w_cat)


def _bias_kernel(tab_ref, o_ref):
    h = pl.program_id(0)
    r = lax.broadcasted_iota(jnp.int32, (TILE, 2 * TILE), 0)
    c = lax.broadcasted_iota(jnp.int32, (TILE, 2 * TILE), 1)
    n = jnp.maximum(r + TILE - c, 0)
    max_exact = N_BUCKETS // 2
    nf = jnp.maximum(n, 1).astype(F32)
    large = max_exact + (jnp.log(nf / max_exact) / math.log(MAX_DISTANCE / max_exact)
                         * (N_BUCKETS - max_exact)).astype(jnp.int32)
    large = jnp.minimum(large, N_BUCKETS - 1)
    bucket = jnp.where(n < max_exact, n, large)
    out = jnp.zeros((TILE, 2 * TILE), F32)
    for b in range(N_BUCKETS):
        out = jnp.where(bucket == b, tab_ref[b, h], out)
    o_ref[0] = out


def _bias_tiles(rel_bias):
    nh = rel_bias.shape[1]
    return pl.pallas_call(
        _bias_kernel,
        grid=(nh,),
        in_specs=[pl.BlockSpec(memory_space=pltpu.SMEM)],
        out_specs=pl.BlockSpec((1, TILE, 2 * TILE), lambda h: (h, 0, 0)),
        out_shape=jax.ShapeDtypeStruct((nh, TILE, 2 * TILE), F32),
        compiler_params=_cparams(("arbitrary",)),
        name="t5_bias_tiles",
    )(rel_bias)


def _moba_sel_kernel(q_ref, k_ref, km_ref, qx_ref, kx_ref):
    i = pl.program_id(1)
    q = q_ref[0]
    k = k_ref[0]
    km = km_ref[0].astype(MXU_DTYPE)
    n_iota = lax.broadcasted_iota(jnp.int32, (TILE, MAX_BLOCKS), 1)
    past = n_iota < i
    past_bias = jnp.where(n_iota <= i, 0.0, NEG)
    onehot = jnp.where(n_iota == i, 1.0, 0.0).astype(kx_ref.dtype)
    for h in range(N_HEADS):
        hs = slice(h * HEAD_DIM, (h + 1) * HEAD_DIM)
        qh = q[:, hs]
        gate = _dot_nt(qh, km[:, hs]) * (1.0 / ATT_SCALE)
        gate = jnp.where(past, gate, NEG)
        choice_bias = jnp.where(n_iota == i, 0.0, NEG)
        for _ in range(MOBA_TOPK):
            m = jnp.max(gate, axis=1, keepdims=True)
            idx = jnp.min(jnp.where(gate == m, n_iota, MAX_BLOCKS), axis=1, keepdims=True)
            hit = n_iota == idx
            choice_bias = jnp.where(hit, past_bias, choice_bias)
            gate = jnp.where(hit, -jnp.inf, gate)
        lo = h * 2 * HEAD_DIM
        qx_ref[0, :, lo:lo + HEAD_DIM] = qh
        qx_ref[0, :, lo + HEAD_DIM:lo + 2 * HEAD_DIM] = choice_bias.astype(qx_ref.dtype)
        kx_ref[0, :, lo:lo + HEAD_DIM] = k[:, hs]
        kx_ref[0, :, lo + HEAD_DIM:lo + 2 * HEAD_DIM] = onehot


def _moba_sel(q, k, kmean_pad):
    B, S, _ = q.shape
    tile = pl.BlockSpec((1, TILE, D_ATT), lambda b, i: (b, i, 0))
    ext = pl.BlockSpec((1, TILE, 2 * D_ATT), lambda b, i: (b, i, 0))
    shp = jax.ShapeDtypeStruct((B, S, 2 * D_ATT), q.dtype)
    return pl.pallas_call(
        _moba_sel_kernel,
        grid=(B, S // TILE),
        in_specs=[tile, tile, pl.BlockSpec((1, MAX_BLOCKS, D_ATT), lambda b, i: (b, 0, 0))],
        out_specs=[ext, ext],
        out_shape=[shp, shp],
        compiler_params=_cparams(("arbitrary", "arbitrary")),
        name="moba_select",
    )(q, k, kmean_pad)


def _pair_tables(n_tiles):
    ii, jj = [], []
    for i in range(n_tiles):
        for j in range(i + 1):
            ii.append(i)
            jj.append(j)
    return jnp.asarray(ii, jnp.int32), jnp.asarray(jj, jnp.int32)


def _flash_step(s, v_h, h, m_sc, l_sc, acc_sc):
    m_prev = m_sc[h]
    m_new = jnp.maximum(m_prev, jnp.max(s, axis=1, keepdims=True))
    alpha = jnp.exp(m_prev - m_new)
    p = jnp.exp(s - m_new)
    l_sc[h] = alpha * l_sc[h] + jnp.sum(p, axis=1, keepdims=True)
    acc_sc[h] = alpha * acc_sc[h] + _dot(p.astype(v_h.dtype), v_h)
    m_sc[h] = m_new


def _attn_kernel(it, jt, q_ref, k_ref, v_ref, *rest, width, use_mask):
    if use_mask:
        mb_ref, bias_ref, c_ref, o_ref, m_sc, l_sc, acc_sc = rest
    else:
        bias_ref, c_ref, o_ref, m_sc, l_sc, acc_sc = rest
    p = pl.program_id(1)
    i = it[p]
    j = jt[p]
    rel = i - j

    @pl.when(j == 0)
    def _():
        m_sc[...] = jnp.full(m_sc.shape, M_INIT, F32)
        l_sc[...] = jnp.zeros(l_sc.shape, F32)
        acc_sc[...] = jnp.zeros(acc_sc.shape, F32)

    def heads(kind):
        mb = mb_ref[0].astype(F32) if use_mask else None
        for h in range(N_HEADS):
            qs = slice(h * width, (h + 1) * width)
            s = _dot_nt(q_ref[0, :, qs], k_ref[0, :, qs])
            if kind == "far":
                s = s + c_ref[h]
            elif kind == "prev":
                s = s + bias_ref[h, :, 0:TILE]
            else:
                s = s + bias_ref[h, :, TILE:2 * TILE]
            if use_mask:
                s = s + mb
            if kind == "own":
                r = lax.broadcasted_iota(jnp.int32, (TILE, TILE), 0)
                c = lax.broadcasted_iota(jnp.int32, (TILE, TILE), 1)
                s = jnp.where(r >= c, s, NEG)
            _flash_step(s, v_ref[0, :, h * HEAD_DIM:(h + 1) * HEAD_DIM], h, m_sc, l_sc, acc_sc)

    @pl.when(rel >= 2)
    def _():
        heads("far")

    @pl.when(rel == 1)
    def _():
        heads("prev")

    @pl.when(rel == 0)
    def _():
        heads("own")
        for h in range(N_HEADS):
            o_ref[0, :, h * HEAD_DIM:(h + 1) * HEAD_DIM] = (acc_sc[h] / l_sc[h]).astype(o_ref.dtype)


def _attention(q, k, v, mask_bias, bias_tiles, far_bias, width):
    B, S, _ = q.shape
    nt = S // TILE
    it, jt = _pair_tables(nt)
    use_mask = mask_bias is not None
    qspec = pl.BlockSpec((1, TILE, N_HEADS * width), lambda b, p, it, jt: (b, it[p], 0))
    kspec = pl.BlockSpec((1, TILE, N_HEADS * width), lambda b, p, it, jt: (b, jt[p], 0))
    vspec = pl.BlockSpec((1, TILE, D_ATT), lambda b, p, it, jt: (b, jt[p], 0))
    in_specs = [qspec, kspec, vspec]
    args = [q, k, v]
    if use_mask:
        in_specs.append(pl.BlockSpec((1, TILE, TILE), lambda b, p, it, jt: (b, it[p], jt[p])))
        args.append(mask_bias)
    in_specs += [pl.BlockSpec(bias_tiles.shape, lambda b, p, it, jt: (0, 0, 0)),
                 pl.BlockSpec(memory_space=pltpu.SMEM)]
    args += [bias_tiles, far_bias]
    return pl.pallas_call(
        functools.partial(_attn_kernel, width=width, use_mask=use_mask),
        grid_spec=pltpu.PrefetchScalarGridSpec(
            num_scalar_prefetch=2,
            grid=(B, it.shape[0]),
            in_specs=in_specs,
            out_specs=pl.BlockSpec((1, TILE, D_ATT), lambda b, p, it, jt: (b, it[p], 0)),
            scratch_shapes=[pltpu.VMEM((N_HEADS, TILE, 1), F32),
                            pltpu.VMEM((N_HEADS, TILE, 1), F32),
                            pltpu.VMEM((N_HEADS, TILE, HEAD_DIM), F32)],
        ),
        out_shape=jax.ShapeDtypeStruct((B, S, D_ATT), v.dtype),
        compiler_params=_cparams(("arbitrary", "arbitrary")),
        name="dsa_attention" if use_mask else "moba_attention",
    )(it, jt, *args)


def _dsa_sel_kernel(qi_ref, wi_ref, kit_ref, mb_ref, keys, wb, plim, *, n_keep, seq):
    i = pl.program_id(1)
    n_chunks = (i * TILE + TILE + KEY_CHUNK - 1) // KEY_CHUNK
    sub = KEY_CHUNK // 128
    t = i * TILE + lax.broadcasted_iota(jnp.int32, (TILE, KEY_CHUNK), 0)
    lane = lax.broadcasted_iota(jnp.int32, (TILE, KEY_CHUNK), 1)

    w = wi_ref[0]
    for h in range(N_IDX_HEADS):
        wb[h] = jnp.broadcast_to(w[:, h:h + 1], (TILE, 128))

    def chunk_start(c):
        return pl.multiple_of(c * KEY_CHUNK, KEY_CHUNK)

    def score_chunk(c, carry):
        c0 = chunk_start(c)
        kc = kit_ref[0, :, pl.ds(c0, KEY_CHUNK)]
        acc = [jnp.zeros((TILE, 128), F32) for _ in range(sub)]
        for h in range(N_IDX_HEADS):
            d = jnp.maximum(_dot(qi_ref[0, :, h * IDX_DIM:(h + 1) * IDX_DIM], kc), 0.0)
            wh = wb[h]
            for u in range(sub):
                acc[u] = acc[u] + wh * d[:, u * 128:(u + 1) * 128]
        score = jnp.concatenate(acc, axis=1) * IDX_SCALE
        score = jnp.where(score == 0.0, 0.0, score)
        score = jnp.where(c0 + lane <= t, score, NEG)
        bits = pltpu.bitcast(score, jnp.int32)
        keys[:, pl.ds(c0, KEY_CHUNK)] = jnp.where(bits < 0, bits ^ 0x7FFFFFFF, bits)
        return carry

    lax.fori_loop(0, n_chunks, score_chunk, 0)

    def count(pred):
        def body(c, acc):
            c0 = chunk_start(c)
            ones = jnp.where(pred(keys[:, pl.ds(c0, KEY_CHUNK)], c0), 1.0, 0.0)
            for u in range(sub):
                acc = acc + ones[:, u * 128:(u + 1) * 128]
            return acc
        acc = lax.fori_loop(0, n_chunks, body, jnp.zeros((TILE, 128), F32))
        return jnp.sum(acc, axis=1, keepdims=True)

    keep = float(n_keep)

    def bit_step(b, thr_u):
        cand_u = thr_u | jnp.left_shift(jnp.int32(1), 31 - b)
        cand_s = cand_u ^ INT_MIN
        cnt = count(lambda kk, c0: kk >= cand_s)
        return jnp.where(cnt >= keep, cand_u, thr_u)

    thr = lax.fori_loop(0, 32, bit_step, jnp.zeros((TILE, 1), jnp.int32)) ^ INT_MIN
    cnt_ge = count(lambda kk, c0: kk >= thr)
    cnt_gt = count(lambda kk, c0: kk > thr)
    need = keep - cnt_gt
    crowded = cnt_ge > keep
    plim[...] = jnp.full((TILE, 1), seq, jnp.int32)

    @pl.when(jnp.max(cnt_ge) > keep)
    def _():
        n_bits = int(math.log2(seq))

        def idx_step(b, lim):
            cand = lim | jnp.left_shift(jnp.int32(1), n_bits - 1 - b)
            cnt = count(lambda kk, c0: (kk == thr) & (c0 + lane < cand))
            return jnp.where(cnt < need, cand, lim)

        lim = lax.fori_loop(0, n_bits, idx_step, jnp.zeros((TILE, 1), jnp.int32))
        plim[...] = jnp.where(crowded, lim, seq)

    lim = plim[...]

    def write_chunk(c, carry):
        c0 = chunk_start(c)
        kk = keys[:, pl.ds(c0, KEY_CHUNK)]
        col = c0 + lane
        chosen = (kk > thr) | ((kk == thr) & (col <= lim))
        chosen = chosen & (col <= t)
        mb_ref[0, :, pl.ds(c0, KEY_CHUNK)] = jnp.where(chosen, 0.0, NEG).astype(mb_ref.dtype)
        return carry

    lax.fori_loop(0, n_chunks, write_chunk, 0)

    def fill_chunk(c, carry):
        mb_ref[0, :, pl.ds(chunk_start(c), KEY_CHUNK)] = jnp.full((TILE, KEY_CHUNK), NEG, mb_ref.dtype)
        return carry

    lax.fori_loop(n_chunks, seq // KEY_CHUNK, fill_chunk, 0)


def _dsa_sel(qi, wi, ki_t, n_keep):
    B, S, _ = qi.shape
    return pl.pallas_call(
        functools.partial(_dsa_sel_kernel, n_keep=n_keep, seq=S),
        grid=(B, S // TILE),
        in_specs=[pl.BlockSpec((1, TILE, N_IDX_HEADS * IDX_DIM), lambda b, i: (b, i, 0)),
                  pl.BlockSpec((1, TILE, N_IDX_HEADS), lambda b, i: (b, i, 0)),
                  pl.BlockSpec((1, IDX_DIM, S), lambda b, i: (b, 0, 0))],
        out_specs=pl.BlockSpec((1, TILE, S), lambda b, i: (b, i, 0)),
        out_shape=jax.ShapeDtypeStruct((B, S, S), jnp.bfloat16),
        scratch_shapes=[pltpu.VMEM((TILE, S), jnp.int32),
                        pltpu.VMEM((N_IDX_HEADS, TILE, 128), F32),
                        pltpu.VMEM((TILE, 1), jnp.int32)],
        compiler_params=_cparams(("arbitrary", "arbitrary")),
        name="dsa_select",
    )(qi, wi, ki_t)


def _gelu_tanh(x):
    return 0.5 * x * (1.0 + jnp.tanh(math.sqrt(2.0 / math.pi) * (x + 0.044715 * (x * x * x))))


def _rnn_kernel(xc_ref, yc_ref, cw_ref, cb_ref, wr_ref, br_ref, wi_ref, bi_ref, lam_ref, o_ref, xbuf, h_sc):
    L = xc_ref.shape[1]
    pad = 8

    @pl.when(pl.program_id(1) == 0)
    def _():
        xbuf[0:pad, :] = jnp.zeros((pad, D_RNN), F32)
        h_sc[...] = jnp.zeros(h_sc.shape, F32)

    x = xc_ref[0]
    xbuf[pad:pad + L, :] = x
    xconv = cb_ref[...]
    for tap in range(CONV_WIDTH):
        back = CONV_WIDTH - 1 - tap
        xconv = xconv + cw_ref[tap:tap + 1, :] * xbuf[pad - back:pad - back + L, :]
    xbuf[0:pad, :] = xbuf[L:L + pad, :]

    xb = xconv.astype(MXU_DTYPE)
    r = jax.nn.sigmoid(_dot(xb, wr_ref[...]) + br_ref[...])
    gi = jax.nn.sigmoid(_dot(xb, wi_ref[...]) + bi_ref[...])
    neg_lam = -lam_ref[...]
    softplus = jnp.maximum(neg_lam, 0.0) + jnp.log1p(jnp.exp(-jnp.abs(neg_lam)))
    log_a = -LRU_C * r * softplus
    a = jnp.exp(log_a)
    u = jnp.sqrt(-jnp.tanh(log_a) * (a * a + 1.0)) * (gi * xconv)

    row = lax.broadcasted_iota(jnp.int32, (L, D_RNN), 0)
    d = 1
    while d < L:
        keep = row >= d
        u = u + a * jnp.where(keep, pltpu.roll(u, d, 0), 0.0)
        a = a * jnp.where(keep, pltpu.roll(a, d, 0), 1.0)
        d *= 2
    hseq = u + a * h_sc[...]
    h_sc[...] = hseq[L - 1:L, :]
    o_ref[0] = (hseq * _gelu_tanh(yc_ref[0])).astype(o_ref.dtype)


def _rnn(xc, yc, conv_w, conv_b, wr_bd, b_r, wi_bd, b_i, lam):
    B, S, _ = xc.shape
    L = TILE
    tile = pl.BlockSpec((1, L, D_RNN), lambda b, s: (b, s, 0))
    vec = _resident((1, D_RNN))
    return pl.pallas_call(
        _rnn_kernel,
        grid=(B, S // L),
        in_specs=[tile, tile, _resident((CONV_WIDTH, D_RNN)), vec, _resident((D_RNN, D_RNN)), vec,
                  _resident((D_RNN, D_RNN)), vec, vec],
        out_specs=tile,
        out_shape=jax.ShapeDtypeStruct((B, S, D_RNN), MXU_DTYPE),
        scratch_shapes=[pltpu.VMEM((L + 8, D_RNN), F32), pltpu.VMEM((1, D_RNN), F32)],
        compiler_params=_cparams(("arbitrary", "arbitrary")),
        name="conv_rglru",
    )(xc, yc, conv_w, conv_b, wr_bd, b_r, wi_bd, b_i, lam)


def _post_kernel(x_ref, ya_ref, yb_ref, yc_ref, gt_ref, wpa, wpb, wpc, wo, g_mlp, w_up, w_down, g_fin, o_ref,
                 *, final):
    D = x_ref.shape[1]
    merged = jax.nn.sigmoid(gt_ref[:, 0:D]) * _dot(ya_ref[...], wpa[...])
    merged = merged + jax.nn.sigmoid(gt_ref[:, D:2 * D]) * _dot(yb_ref[...], wpb[...])
    merged = merged + jax.nn.sigmoid(gt_ref[:, 2 * D:3 * D]) * _dot(yc_ref[...], wpc[...])
    x1 = x_ref[...] + _dot(merged.astype(MXU_DTYPE), wo[...])
    xn = _rms(x1, g_mlp[...]).astype(MXU_DTYPE)
    d_ff = w_up.shape[1]
    acc = x1
    for c in range(d_ff // 1024):
        hid = jnp.maximum(_dot(xn, w_up[:, c * 1024:(c + 1) * 1024]), 0.0)
        acc = acc + _dot((hid * hid).astype(MXU_DTYPE), w_down[c * 1024:(c + 1) * 1024, :])
    if final:
        acc = _rms(acc, g_fin[...])
    o_ref[...] = acc


def _post(x2, ya, yb, yc, gates, wpa, wpb, wpc, wo, g_mlp, w_up, w_down, g_fin, final):
    T, D = x2.shape
    row = lambda w: pl.BlockSpec((TILE, w), lambda t: (t, 0))
    weights = (wpa, wpb, wpc, wo, g_mlp, w_up, w_down, g_fin)
    return pl.pallas_call(
        functools.partial(_post_kernel, final=final),
        grid=(T // TILE,),
        in_specs=[row(D), row(D_ATT), row(D_ATT), row(D_RNN), row(3 * D)] + [_resident(w.shape) for w in weights],
        out_specs=row(D),
        out_shape=jax.ShapeDtypeStruct((T, D), F32),
        compiler_params=_cparams(("arbitrary",)),
        name="merge_mlp",
    )(x2, ya, yb, yc, gates, *weights)


def _block_diag(w):
    return jax.scipy.linalg.block_diag(*[w[n] for n in range(w.shape[0])])


def kernel(x, rel_bias, norm_mix_g, w_in, conv_w, conv_b, w_r, b_r, w_i, b_i, lru_lambda, w_pa, w_pb, w_pc, w_o,
           norm_mlp_g, w_up, w_down, final_norm_g):
    B, S, D = x.shape
    depth = w_in.shape[0]
    nb = S // TILE
    assert S % KEY_CHUNK == 0 and nb <= MAX_BLOCKS and S & (S - 1) == 0
    n_keep = min(DSA_MAX_TOPK, S // 4)
    cast = lambda a: a.astype(MXU_DTYPE)
    vec = lambda a: a.reshape(1, -1)

    bias_tiles = _bias_tiles(rel_bias)
    far_bias = rel_bias[N_BUCKETS - 1]
    x2 = x.reshape(B * S, D)
    for l in range(depth):
        w = w_in[l]
        n_small = IDX_DIM + N_IDX_HEADS
        w_cat = cast(jnp.concatenate(
            [w[:, :N_ATT_COLS + n_small], jnp.zeros((D, SMALL_COLS - n_small), w.dtype),
             w[:, N_ATT_COLS + n_small:]], axis=1))
        (qa, ka, va, qb, kb, vb, qi, ki, wi, xc, yc, gates, kmean) = _inproj(x2, vec(norm_mix_g[l]), w_cat)
        seq = lambda a: a.reshape(B, S, a.shape[-1])

        kmean_pad = jnp.pad(kmean.reshape(B, nb, D_ATT), ((0, 0), (0, MAX_BLOCKS - nb), (0, 0)))
        qx, kx = _moba_sel(seq(qa), seq(ka), kmean_pad)
        y_a = _attention(qx, kx, seq(va), None, bias_tiles[:N_HEADS], far_bias[:N_HEADS], 2 * HEAD_DIM)

        mask_bias = _dsa_sel(seq(qi), seq(wi), jnp.swapaxes(seq(ki), 1, 2), n_keep)
        y_b = _attention(seq(qb), seq(kb), seq(vb), mask_bias, bias_tiles[N_HEADS:], far_bias[N_HEADS:], HEAD_DIM)

        y_c = _rnn(seq(xc), seq(yc), conv_w[l], vec(conv_b[l]), cast(_block_diag(w_r[l])), vec(b_r[l]),
                   cast(_block_diag(w_i[l])), vec(b_i[l]), vec(lru_lambda[l]))

        x2 = _post(x2, y_a.reshape(B * S, D_ATT), y_b.reshape(B * S, D_ATT), y_c.reshape(B * S, D_RNN), gates,
                   cast(w_pa[l]), cast(w_pb[l]), cast(w_pc[l]), cast(w_o[l]), vec(norm_mlp_g[l]),
                   cast(w_up[l]), cast(w_down[l]), vec(final_norm_g), final=(l == depth - 1))
    return x2.reshape(B, S, D)
```

```python
import functools
import math

import numpy as np
import jax
import jax.numpy as jnp
from jax import lax
from jax.experimental import pallas as pl
from jax.experimental.pallas import tpu as pltpu

F32 = jnp.float32
MXU_DTYPE = jnp.bfloat16

HEAD_DIM = 64
N_HEADS = 8
D_ATT = N_HEADS * HEAD_DIM
IDX_DIM = 64
N_IDX_HEADS = 8
D_RNN = 512
CONV_WIDTH = 4
LRU_C = 8.0
MOBA_TOPK = 3
DSA_MAX_TOPK = 256
N_BUCKETS = 32
MAX_DISTANCE = 128
EPS = 1e-6
NEG = -1e30
M_INIT = -1e29
ATT_SCALE = HEAD_DIM ** -0.5
IDX_SCALE = (N_IDX_HEADS ** -0.5) * (IDX_DIM ** -0.5)

TILE = 256
Q_TILE = 2 * TILE
HEAD_LANES = 2 * HEAD_DIM
MAX_BLOCKS = 64
VMEM_LIMIT = 56 * 1024 * 1024
INT_MIN = -2 ** 31


def _cparams(sem):
    return pltpu.CompilerParams(dimension_semantics=sem, vmem_limit_bytes=VMEM_LIMIT)


def _resident(shape):
    nd = len(shape)
    return pl.BlockSpec(shape, lambda *_: (0,) * nd, pipeline_mode=pl.Buffered(1))


def _rms(x, g):
    return x * lax.rsqrt(jnp.mean(x * x, axis=-1, keepdims=True) + EPS) * g


def _dot(a, b):
    return jnp.dot(a, b, preferred_element_type=F32)


def _dot_nt(a, b):
    return lax.dot_general(a, b, (((1,), (1,)), ((), ())), preferred_element_type=F32)


N_ATT_COLS = 7 * D_ATT
SMALL_COLS = 128
W_CAT_COLS = N_ATT_COLS + SMALL_COLS + 2 * D_RNN + 3 * 1024


def _inproj_kernel(x_ref, g_ref, w_ref, qa, ka, va, qb, kb, vb, qi, ki, wi, xc, yc, gt, kmean):
    xn = _rms(x_ref[...], g_ref[...]).astype(MXU_DTYPE)

    def mm(lo, width):
        return _dot(xn, w_ref[:, lo:lo + width])

    for n, o in enumerate((qa, ka, va, qb, kb, vb, qi)):
        r = mm(n * D_ATT, D_ATT)
        if n in (0, 3):
            r = r * ATT_SCALE
        if n == 1:
            kmean[0] = jnp.mean(r, axis=0, keepdims=True)
        o[...] = r.astype(o.dtype)
    r = mm(N_ATT_COLS, SMALL_COLS)
    ki[...] = r[:, :IDX_DIM].astype(ki.dtype)
    wi[...] = r[:, IDX_DIM:IDX_DIM + N_IDX_HEADS]
    lo = N_ATT_COLS + SMALL_COLS
    xc[...] = mm(lo, D_RNN)
    yc[...] = mm(lo + D_RNN, D_RNN)
    lo += 2 * D_RNN
    for c in range(3):
        gt[:, c * 1024:(c + 1) * 1024] = mm(lo + c * 1024, 1024)


def _inproj(x2, g, w_cat):
    T, D = x2.shape
    nt = T // TILE
    row = lambda w: pl.BlockSpec((TILE, w), lambda t: (t, 0))
    att = jax.ShapeDtypeStruct((T, D_ATT), MXU_DTYPE)
    out_shape = [att] * 7 + [
        jax.ShapeDtypeStruct((T, IDX_DIM), MXU_DTYPE),
        jax.ShapeDtypeStruct((T, N_IDX_HEADS), F32),
        jax.ShapeDtypeStruct((T, D_RNN), F32),
        jax.ShapeDtypeStruct((T, D_RNN), F32),
        jax.ShapeDtypeStruct((T, 3 * D), F32),
        jax.ShapeDtypeStruct((nt, 1, D_ATT), F32),
    ]
    out_specs = [row(D_ATT)] * 7 + [row(IDX_DIM), row(N_IDX_HEADS), row(D_RNN), row(D_RNN), row(3 * D),
                                    pl.BlockSpec((1, 1, D_ATT), lambda t: (t, 0, 0))]
    return pl.pallas_call(
        _inproj_kernel,
        grid=(nt,),
        in_specs=[row(D), _resident((1, D)), _resident(w_cat.shape)],
        out_specs=out_specs,
        out_shape=out_shape,
        compiler_params=_cparams(("arbitrary",)),
        name="inproj",
    )(x2, g, w_cat)


N_NEAR = 3


def _bias_kernel(tab_ref, o_ref):
    w = pl.program_id(0)
    h = pl.program_id(1)
    key = lax.broadcasted_iota(jnp.int32, (TILE, Q_TILE), 0)
    qry = lax.broadcasted_iota(jnp.int32, (TILE, Q_TILE), 1)
    dist = TILE * (w - 1) + qry - key
    n = jnp.maximum(dist, 0)
    max_exact = N_BUCKETS // 2
    nf = jnp.maximum(n, 1).astype(F32)
    large = max_exact + (jnp.log(nf / max_exact) / math.log(MAX_DISTANCE / max_exact)
                         * (N_BUCKETS - max_exact)).astype(jnp.int32)
    large = jnp.minimum(large, N_BUCKETS - 1)
    bucket = jnp.where(n < max_exact, n, large)
    out = jnp.zeros((TILE, Q_TILE), F32)
    for b in range(N_BUCKETS):
        out = jnp.where(bucket == b, tab_ref[b, h], out)
    o_ref[0, 0] = jnp.where(dist >= 0, out - tab_ref[N_BUCKETS - 1, h], NEG)


def _bias_tiles(rel_bias):
    nh = rel_bias.shape[1]
    return pl.pallas_call(
        _bias_kernel,
        grid=(N_NEAR, nh),
        in_specs=[pl.BlockSpec(memory_space=pltpu.SMEM)],
        out_specs=pl.BlockSpec((1, 1, TILE, Q_TILE), lambda w, h: (w, h, 0, 0)),
        out_shape=jax.ShapeDtypeStruct((N_NEAR, nh, TILE, Q_TILE), F32),
        compiler_params=_cparams(("arbitrary", "arbitrary")),
        name="t5_bias_tiles",
    )(rel_bias)


def _moba_sel_kernel(q_ref, k_ref, km_ref, qx_ref, kx_ref):
    i = pl.program_id(1)
    q = q_ref[0]
    k = k_ref[0]
    km = km_ref[0].astype(MXU_DTYPE)
    n_iota = lax.broadcasted_iota(jnp.int32, (TILE, MAX_BLOCKS), 1)
    past = n_iota < i
    past_bias = jnp.where(n_iota <= i, 0.0, NEG)
    onehot = jnp.where(n_iota == i, 1.0, 0.0).astype(kx_ref.dtype)
    for h in range(N_HEADS):
        hs = slice(h * HEAD_DIM, (h + 1) * HEAD_DIM)
        qh = q[:, hs]
        gate = _dot_nt(qh, km[:, hs]) * (1.0 / ATT_SCALE)
        gate = jnp.where(past, gate, NEG)
        choice_bias = jnp.where(n_iota == i, 0.0, NEG)
        for _ in range(MOBA_TOPK):
            m = jnp.max(gate, axis=1, keepdims=True)
            idx = jnp.min(jnp.where(gate == m, n_iota, MAX_BLOCKS), axis=1, keepdims=True)
            hit = n_iota == idx
            choice_bias = jnp.where(hit, past_bias, choice_bias)
            gate = jnp.where(hit, -jnp.inf, gate)
        lo = h * 2 * HEAD_DIM
        qx_ref[0, :, lo:lo + HEAD_DIM] = qh
        qx_ref[0, :, lo + HEAD_DIM:lo + 2 * HEAD_DIM] = choice_bias.astype(qx_ref.dtype)
        kx_ref[0, :, lo:lo + HEAD_DIM] = k[:, hs]
        kx_ref[0, :, lo + HEAD_DIM:lo + 2 * HEAD_DIM] = onehot


def _moba_sel(q, k, kmean_pad):
    B, S, _ = q.shape
    tile = pl.BlockSpec((1, TILE, D_ATT), lambda b, i: (b, i, 0))
    ext = pl.BlockSpec((1, TILE, 2 * D_ATT), lambda b, i: (b, i, 0))
    shp = jax.ShapeDtypeStruct((B, S, 2 * D_ATT), q.dtype)
    return pl.pallas_call(
        _moba_sel_kernel,
        grid=(B, S // TILE),
        in_specs=[tile, tile, pl.BlockSpec((1, MAX_BLOCKS, D_ATT), lambda b, i: (b, 0, 0))],
        out_specs=[ext, ext],
        out_shape=[shp, shp],
        compiler_params=_cparams(("arbitrary", "arbitrary")),
        name="moba_select",
    )(q, k, kmean_pad)


def _pair_tables(n_q_tiles):
    per = Q_TILE // TILE
    ii, jj = [], []
    for i in range(n_q_tiles):
        for j in range(per * (i + 1)):
            ii.append(i)
            jj.append(j)
    return jnp.asarray(ii, jnp.int32), jnp.asarray(jj, jnp.int32)


def _attn_kernel(it, jt, qt_ref, k_ref, vt_ref, *rest, use_mask):
    if use_mask:
        mask_ref, bias_ref, o_ref, m_sc, l_sc, acc_sc, mask_sc = rest
    else:
        bias_ref, o_ref, m_sc, l_sc, acc_sc = rest
    p = pl.program_id(1)
    i = it[p]
    j = jt[p]
    per = Q_TILE // TILE
    lead = per * i - j

    @pl.when(j == 0)
    def _():
        m_sc[...] = jnp.full(m_sc.shape, M_INIT, F32)
        l_sc[...] = jnp.zeros(l_sc.shape, F32)
        acc_sc[...] = jnp.zeros(acc_sc.shape, F32)

    if use_mask:
        mask_sc[...] = mask_ref[0].astype(F32)

    def heads(near):
        for h in range(N_HEADS):
            s = _dot(k_ref[0, :, h * HEAD_LANES:(h + 1) * HEAD_LANES], qt_ref[0, h])
            if near:
                s = s + bias_ref[lead + 1, h]
            if use_mask:
                s = s + mask_sc[...]
            m_prev = m_sc[h]
            m_new = jnp.maximum(m_prev, jnp.max(s, axis=0, keepdims=True))
            alpha = jnp.exp(m_prev - m_new)
            prob = jnp.exp(s - m_new)
            l_sc[h] = alpha * l_sc[h] + jnp.sum(prob, axis=0, keepdims=True)
            acc_sc[h] = alpha * acc_sc[h] + _dot(vt_ref[0, h], prob.astype(vt_ref.dtype))
            m_sc[h] = m_new

    @pl.when(lead >= N_NEAR - 1)
    def _():
        heads(False)

    @pl.when(lead < N_NEAR - 1)
    def _():
        heads(True)

    @pl.when(j == per * (i + 1) - 1)
    def _():
        for h in range(N_HEADS):
            o_ref[0, h] = (acc_sc[h] / l_sc[h]).astype(o_ref.dtype)


def _attention(qt, k, vt, mask_t, bias_tiles):
    B, _, _, S = qt.shape
    it, jt = _pair_tables(S // Q_TILE)
    use_mask = mask_t is not None
    in_specs = [pl.BlockSpec((1, N_HEADS, HEAD_LANES, Q_TILE), lambda b, p, it, jt: (b, 0, 0, it[p])),
                pl.BlockSpec((1, TILE, N_HEADS * HEAD_LANES), lambda b, p, it, jt: (b, jt[p], 0)),
                pl.BlockSpec((1, N_HEADS, HEAD_DIM, TILE), lambda b, p, it, jt: (b, 0, 0, jt[p]))]
    args = [qt, k, vt]
    scratch = [pltpu.VMEM((N_HEADS, 1, Q_TILE), F32),
               pltpu.VMEM((N_HEADS, 1, Q_TILE), F32),
               pltpu.VMEM((N_HEADS, HEAD_DIM, Q_TILE), F32)]
    if use_mask:
        in_specs.append(pl.BlockSpec((1, TILE, Q_TILE), lambda b, p, it, jt: (b, jt[p], it[p])))
        args.append(mask_t)
        scratch.append(pltpu.VMEM((TILE, Q_TILE), F32))
    in_specs.append(pl.BlockSpec(bias_tiles.shape, lambda b, p, it, jt: (0, 0, 0, 0), pipeline_mode=pl.Buffered(1)))
    args.append(bias_tiles)
    return pl.pallas_call(
        functools.partial(_attn_kernel, use_mask=use_mask),
        grid_spec=pltpu.PrefetchScalarGridSpec(
            num_scalar_prefetch=2,
            grid=(B, it.shape[0]),
            in_specs=in_specs,
            out_specs=pl.BlockSpec((1, N_HEADS, HEAD_DIM, Q_TILE), lambda b, p, it, jt: (b, 0, 0, it[p])),
            scratch_shapes=scratch,
        ),
        out_shape=jax.ShapeDtypeStruct((B, N_HEADS, HEAD_DIM, S), vt.dtype),
        compiler_params=_cparams(("arbitrary", "arbitrary")),
        name="dsa_attention" if use_mask else "moba_attention",
    )(it, jt, *args)


SCORE_ROWS = 512
COUNT_ROWS = 512
GROUP_ROWS = 256


def _order_key(score):
    bits = pltpu.bitcast(score, jnp.int32)
    return jnp.where(bits < 0, bits ^ 0x7FFFFFFF, bits)


def _dsa_sel_kernel(qit_ref, wit_ref, ki_ref, mask_ref, keys, gmax, lim_sc, *, n_keep, seq):
    i = pl.program_id(1)
    n_count = (i * TILE + TILE + COUNT_ROWS - 1) // COUNT_ROWS
    q_pos = i * TILE + lax.broadcasted_iota(jnp.int32, (1, TILE), 1)
    w = wit_ref[0]
    keep = float(n_keep)

    gmax[...] = jnp.full(gmax.shape, NEG, F32)

    def score_chunk(c, carry):
        c0 = pl.multiple_of(c * SCORE_ROWS, SCORE_ROWS)
        kc = ki_ref[0, pl.ds(c0, SCORE_ROWS), :]
        acc = jnp.zeros((SCORE_ROWS, TILE), F32)
        for h in range(N_IDX_HEADS):
            acc = acc + w[h:h + 1, :] * jnp.maximum(_dot(kc, qit_ref[0, h]), 0.0)
        score = acc * IDX_SCALE
        score = jnp.where(score == 0.0, 0.0, score)
        key_pos = c0 + lax.broadcasted_iota(jnp.int32, (SCORE_ROWS, TILE), 0)
        score = jnp.where(key_pos <= q_pos, score, NEG)
        keys[pl.ds(c0, SCORE_ROWS), :] = _order_key(score)
        for g0 in range(0, SCORE_ROWS, GROUP_ROWS):
            gmax[...] = jnp.maximum(gmax[...], score[g0:g0 + GROUP_ROWS, :])
        return carry

    lax.fori_loop(0, n_count * (COUNT_ROWS // SCORE_ROWS), score_chunk, 0)

    def count(ones_fn):
        def body(c, acc):
            c0 = pl.multiple_of(c * COUNT_ROWS, COUNT_ROWS)
            key_pos = c0 + lax.broadcasted_iota(jnp.int32, (COUNT_ROWS, TILE), 0)
            ones = ones_fn(keys[pl.ds(c0, COUNT_ROWS), :], key_pos)
            return acc + jnp.sum(ones.reshape(COUNT_ROWS // 8, 8, TILE), axis=0)
        acc = lax.fori_loop(0, n_count, body, jnp.zeros((8, TILE), jnp.int32))
        return jnp.sum(acc.astype(F32), axis=0, keepdims=True)

    groups = gmax[...]
    lo_u = _order_key(jnp.min(groups, axis=0, keepdims=True)) ^ INT_MIN
    hi_u = _order_key(jnp.max(groups, axis=0, keepdims=True)) ^ INT_MIN
    n_bits = jnp.max(32 - lax.clz(lo_u ^ hi_u))
    low_mask = jnp.where(n_bits >= 32, jnp.int32(-1), jnp.left_shift(jnp.int32(1), n_bits) - 1)

    def bit_step(b, thr_u):
        cand_u = thr_u | jnp.left_shift(jnp.int32(1), n_bits - 1 - b)
        cand_s = cand_u ^ INT_MIN
        cnt = count(lambda kk, pos: jnp.where(kk >= cand_s, 1, 0))
        return jnp.where(cnt >= keep, cand_u, thr_u)

    thr = lax.fori_loop(0, n_bits, bit_step, lo_u & ~low_mask) ^ INT_MIN
    cnt_ge = count(lambda kk, pos: jnp.where(kk >= thr, 1, 0))
    cnt_gt = count(lambda kk, pos: jnp.where(kk > thr, 1, 0))
    need = keep - cnt_gt
    crowded = cnt_ge > keep
    any_crowded = jnp.max(cnt_ge) > keep

    def write(chosen_fn):
        def body(c, carry):
            c0 = pl.multiple_of(c * COUNT_ROWS, COUNT_ROWS)
            key_pos = c0 + lax.broadcasted_iota(jnp.int32, (COUNT_ROWS, TILE), 0)
            bias = chosen_fn(keys[pl.ds(c0, COUNT_ROWS), :], key_pos)
            mask_ref[0, pl.ds(c0, COUNT_ROWS), :] = jnp.where(key_pos <= q_pos, bias, NEG).astype(mask_ref.dtype)
            return carry
        lax.fori_loop(0, n_count, body, 0)

    @pl.when(jnp.logical_not(any_crowded))
    def _():
        write(lambda kk, pos: jnp.where(kk >= thr, 0.0, NEG))

    @pl.when(any_crowded)
    def _():
        n_idx_bits = int(math.log2(seq))

        def idx_step(b, lim):
            cand = lim | jnp.left_shift(jnp.int32(1), n_idx_bits - 1 - b)
            cnt = count(lambda kk, pos: jnp.where(kk == thr, jnp.where(pos < cand, 1, 0), 0))
            return jnp.where(cnt < need, cand, lim)

        lim = lax.fori_loop(0, n_idx_bits, idx_step, jnp.zeros((1, TILE), jnp.int32))
        lim_sc[...] = jnp.where(crowded, lim, seq)
        lim = lim_sc[...]
        write(lambda kk, pos: jnp.where(kk > thr, 0.0, jnp.where(kk == thr, jnp.where(pos <= lim, 0.0, NEG), NEG)))

    def fill_chunk(c, carry):
        c0 = pl.multiple_of(c * COUNT_ROWS, COUNT_ROWS)
        mask_ref[0, pl.ds(c0, COUNT_ROWS), :] = jnp.full((COUNT_ROWS, TILE), NEG, mask_ref.dtype)
        return carry

    lax.fori_loop(n_count, seq // COUNT_ROWS, fill_chunk, 0)


def _dsa_sel(qi_t, wi_t, ki, n_keep):
    B, S, _ = ki.shape
    assert n_keep <= GROUP_ROWS and S % COUNT_ROWS == 0
    return pl.pallas_call(
        functools.partial(_dsa_sel_kernel, n_keep=n_keep, seq=S),
        grid=(B, S // TILE),
        in_specs=[pl.BlockSpec((1, N_IDX_HEADS, IDX_DIM, TILE), lambda b, i: (b, 0, 0, i)),
                  pl.BlockSpec((1, N_IDX_HEADS, TILE), lambda b, i: (b, 0, i)),
                  pl.BlockSpec((1, S, IDX_DIM), lambda b, i: (b, 0, 0))],
        out_specs=pl.BlockSpec((1, S, TILE), lambda b, i: (b, 0, i)),
        out_shape=jax.ShapeDtypeStruct((B, S, S), jnp.bfloat16),
        scratch_shapes=[pltpu.VMEM((S, TILE), jnp.int32),
                        pltpu.VMEM((GROUP_ROWS, TILE), F32),
                        pltpu.VMEM((1, TILE), jnp.int32)],
        compiler_params=_cparams(("arbitrary", "arbitrary")),
        name="dsa_select",
    )(qi_t, wi_t, ki)


def _gelu_tanh(x):
    return 0.5 * x * (1.0 + jnp.tanh(math.sqrt(2.0 / math.pi) * (x + 0.044715 * (x * x * x))))


def _rnn_kernel(xc_ref, yc_ref, cw_ref, cb_ref, wr_ref, br_ref, wi_ref, bi_ref, lam_ref, o_ref, xbuf, h_sc):
    L = xc_ref.shape[1]
    pad = 8

    @pl.when(pl.program_id(1) == 0)
    def _():
        xbuf[0:pad, :] = jnp.zeros((pad, D_RNN), F32)
        h_sc[...] = jnp.zeros(h_sc.shape, F32)

    x = xc_ref[0]
    xbuf[pad:pad + L, :] = x
    xconv = cb_ref[...]
    for tap in range(CONV_WIDTH):
        back = CONV_WIDTH - 1 - tap
        xconv = xconv + cw_ref[tap:tap + 1, :] * xbuf[pad - back:pad - back + L, :]
    xbuf[0:pad, :] = xbuf[L:L + pad, :]

    xb = xconv.astype(MXU_DTYPE)
    r = jax.nn.sigmoid(_dot(xb, wr_ref[...]) + br_ref[...])
    gi = jax.nn.sigmoid(_dot(xb, wi_ref[...]) + bi_ref[...])
    neg_lam = -lam_ref[...]
    softplus = jnp.maximum(neg_lam, 0.0) + jnp.log1p(jnp.exp(-jnp.abs(neg_lam)))
    log_a = -LRU_C * r * softplus
    a = jnp.exp(log_a)
    u = jnp.sqrt(-jnp.tanh(log_a) * (a * a + 1.0)) * (gi * xconv)

    row = lax.broadcasted_iota(jnp.int32, (L, D_RNN), 0)
    d = 1
    while d < L:
        keep = row >= d
        u = u + a * jnp.where(keep, pltpu.roll(u, d, 0), 0.0)
        a = a * jnp.where(keep, pltpu.roll(a, d, 0), 1.0)
        d *= 2
    hseq = u + a * h_sc[...]
    h_sc[...] = hseq[L - 1:L, :]
    o_ref[0] = (hseq * _gelu_tanh(yc_ref[0])).astype(o_ref.dtype)


def _rnn(xc, yc, conv_w, conv_b, wr_bd, b_r, wi_bd, b_i, lam):
    B, S, _ = xc.shape
    L = TILE
    tile = pl.BlockSpec((1, L, D_RNN), lambda b, s: (b, s, 0))
    vec = _resident((1, D_RNN))
    return pl.pallas_call(
        _rnn_kernel,
        grid=(B, S // L),
        in_specs=[tile, tile, _resident((CONV_WIDTH, D_RNN)), vec, _resident((D_RNN, D_RNN)), vec,
                  _resident((D_RNN, D_RNN)), vec, vec],
        out_specs=tile,
        out_shape=jax.ShapeDtypeStruct((B, S, D_RNN), MXU_DTYPE),
        scratch_shapes=[pltpu.VMEM((L + 8, D_RNN), F32), pltpu.VMEM((1, D_RNN), F32)],
        compiler_params=_cparams(("arbitrary", "arbitrary")),
        name="conv_rglru",
    )(xc, yc, conv_w, conv_b, wr_bd, b_r, wi_bd, b_i, lam)


def _post_kernel(x_ref, ya_ref, yb_ref, yc_ref, gt_ref, wpa, wpb, wpc, wo, g_mlp, w_up, w_down, g_fin, o_ref,
                 *, final):
    D = x_ref.shape[1]
    merged = jax.nn.sigmoid(gt_ref[:, 0:D]) * _dot(ya_ref[...], wpa[...])
    merged = merged + jax.nn.sigmoid(gt_ref[:, D:2 * D]) * _dot(yb_ref[...], wpb[...])
    merged = merged + jax.nn.sigmoid(gt_ref[:, 2 * D:3 * D]) * _dot(yc_ref[...], wpc[...])
    x1 = x_ref[...] + _dot(merged.astype(MXU_DTYPE), wo[...])
    xn = _rms(x1, g_mlp[...]).astype(MXU_DTYPE)
    d_ff = w_up.shape[1]
    acc = x1
    for c in range(d_ff // 1024):
        hid = jnp.maximum(_dot(xn, w_up[:, c * 1024:(c + 1) * 1024]), 0.0)
        acc = acc + _dot((hid * hid).astype(MXU_DTYPE), w_down[c * 1024:(c + 1) * 1024, :])
    if final:
        acc = _rms(acc, g_fin[...])
    o_ref[...] = acc


def _post(x2, ya, yb, yc, gates, wpa, wpb, wpc, wo, g_mlp, w_up, w_down, g_fin, final):
    T, D = x2.shape
    row = lambda w: pl.BlockSpec((TILE, w), lambda t: (t, 0))
    weights = (wpa, wpb, wpc, wo, g_mlp, w_up, w_down, g_fin)
    return pl.pallas_call(
        functools.partial(_post_kernel, final=final),
        grid=(T // TILE,),
        in_specs=[row(D), row(D_ATT), row(D_ATT), row(D_RNN), row(3 * D)] + [_resident(w.shape) for w in weights],
        out_specs=row(D),
        out_shape=jax.ShapeDtypeStruct((T, D), F32),
        compiler_params=_cparams(("arbitrary",)),
        name="merge_mlp",
    )(x2, ya, yb, yc, gates, *weights)


def _block_diag(w):
    return jax.scipy.linalg.block_diag(*[w[n] for n in range(w.shape[0])])


def kernel(x, rel_bias, norm_mix_g, w_in, conv_w, conv_b, w_r, b_r, w_i, b_i, lru_lambda, w_pa, w_pb, w_pc, w_o,
           norm_mlp_g, w_up, w_down, final_norm_g):
    B, S, D = x.shape
    depth = w_in.shape[0]
    nb = S // TILE
    assert S % Q_TILE == 0 and nb <= MAX_BLOCKS and S & (S - 1) == 0
    n_keep = min(DSA_MAX_TOPK, S // 4)
    cast = lambda a: a.astype(MXU_DTYPE)
    vec = lambda a: a.reshape(1, -1)

    bias_tiles = _bias_tiles(rel_bias)
    x2 = x.reshape(B * S, D)
    seq = lambda a: a.reshape(B, S, a.shape[-1])
    heads = lambda a: a.reshape(B, S, N_HEADS, a.shape[-1] // N_HEADS)
    heads_t = lambda a: jnp.transpose(heads(a), (0, 2, 3, 1))
    pad_heads = lambda a: jnp.pad(heads(a), ((0, 0), (0, 0), (0, 0), (0, HEAD_LANES - HEAD_DIM)))
    tokens = lambda yt: jnp.transpose(yt, (0, 3, 1, 2)).reshape(B * S, D_ATT)
    for l in range(depth):
        w = w_in[l]
        n_small = IDX_DIM + N_IDX_HEADS
        w_cat = cast(jnp.concatenate(
            [w[:, :N_ATT_COLS + n_small], jnp.zeros((D, SMALL_COLS - n_small), w.dtype),
             w[:, N_ATT_COLS + n_small:]], axis=1))
        (qa, ka, va, qb, kb, vb, qi, ki, wi, xc, yc, gates, kmean) = _inproj(x2, vec(norm_mix_g[l]), w_cat)

        kmean_pad = jnp.pad(kmean.reshape(B, nb, D_ATT), ((0, 0), (0, MAX_BLOCKS - nb), (0, 0)))
        qx, kx = _moba_sel(seq(qa), seq(ka), kmean_pad)
        y_a = _attention(heads_t(qx), kx, heads_t(va), None, bias_tiles[:, :N_HEADS])

        mask_t = _dsa_sel(heads_t(qi), jnp.swapaxes(seq(wi), 1, 2), seq(ki), n_keep)
        y_b = _attention(jnp.transpose(pad_heads(qb), (0, 2, 3, 1)), pad_heads(kb).reshape(B, S, -1), heads_t(vb),
                         mask_t, bias_tiles[:, N_HEADS:])

        y_c = _rnn(seq(xc), seq(yc), conv_w[l], vec(conv_b[l]), cast(_block_diag(w_r[l])), vec(b_r[l]),
                   cast(_block_diag(w_i[l])), vec(b_i[l]), vec(lru_lambda[l]))

        x2 = _post(x2, tokens(y_a), tokens(y_b), y_c.reshape(B * S, D_RNN), gates,
                   cast(w_pa[l]), cast(w_pb[l]), cast(w_pc[l]), cast(w_o[l]), vec(norm_mlp_g[l]),
                   cast(w_up[l]), cast(w_down[l]), vec(final_norm_g), final=(l == depth - 1))
    return x2.reshape(B, S, D)
```

```python
import functools
import math

import numpy as np
import jax
import jax.numpy as jnp
from jax import lax
from jax.experimental import pallas as pl
from jax.experimental.pallas import tpu as pltpu

F32 = jnp.float32
MXU_DTYPE = jnp.bfloat16

HEAD_DIM = 64
N_HEADS = 8
D_ATT = N_HEADS * HEAD_DIM
IDX_DIM = 64
N_IDX_HEADS = 8
D_RNN = 512
CONV_WIDTH = 4
LRU_C = 8.0
MOBA_TOPK = 3
DSA_MAX_TOPK = 256
N_BUCKETS = 32
MAX_DISTANCE = 128
EPS = 1e-6
NEG = -1e30
M_INIT = -1e29
ATT_SCALE = HEAD_DIM ** -0.5
IDX_SCALE = (N_IDX_HEADS ** -0.5) * (IDX_DIM ** -0.5)

TILE = 256
Q_TILE = 2 * TILE
HEAD_LANES = 2 * HEAD_DIM
V_ROWS = HEAD_DIM + 16
MAX_BLOCKS = 64
VMEM_LIMIT = 56 * 1024 * 1024
INT_MIN = -2 ** 31


def _cparams(sem):
    return pltpu.CompilerParams(dimension_semantics=sem, vmem_limit_bytes=VMEM_LIMIT)


def _resident(shape):
    nd = len(shape)
    return pl.BlockSpec(shape, lambda *_: (0,) * nd, pipeline_mode=pl.Buffered(1))


def _rms(x, g):
    return x * lax.rsqrt(jnp.mean(x * x, axis=-1, keepdims=True) + EPS) * g


def _dot(a, b):
    return jnp.dot(a, b, preferred_element_type=F32)


def _dot_nt(a, b):
    return lax.dot_general(a, b, (((1,), (1,)), ((), ())), preferred_element_type=F32)


N_ATT_COLS = 7 * D_ATT
SMALL_COLS = 128
W_CAT_COLS = N_ATT_COLS + SMALL_COLS + 2 * D_RNN + 3 * 1024


def _inproj_kernel(x_ref, g_ref, w_ref, qa, ka, va, qb, kb, vb, qi, ki, wi, xc, yc, gt, kmean):
    xn = _rms(x_ref[...], g_ref[...]).astype(MXU_DTYPE)

    def mm(lo, width):
        return _dot(xn, w_ref[:, lo:lo + width])

    for n, o in enumerate((qa, ka, va, qb, kb, vb, qi)):
        r = mm(n * D_ATT, D_ATT)
        if n in (0, 3):
            r = r * ATT_SCALE
        if n == 1:
            kmean[0] = jnp.mean(r, axis=0, keepdims=True)
        o[...] = r.astype(o.dtype)
    r = mm(N_ATT_COLS, SMALL_COLS)
    ki[...] = r[:, :IDX_DIM].astype(ki.dtype)
    wi[...] = r[:, IDX_DIM:IDX_DIM + N_IDX_HEADS]
    lo = N_ATT_COLS + SMALL_COLS
    xc[...] = mm(lo, D_RNN)
    yc[...] = mm(lo + D_RNN, D_RNN)
    lo += 2 * D_RNN
    for c in range(3):
        gt[:, c * 1024:(c + 1) * 1024] = mm(lo + c * 1024, 1024)


def _inproj(x2, g, w_cat):
    T, D = x2.shape
    nt = T // TILE
    row = lambda w: pl.BlockSpec((TILE, w), lambda t: (t, 0))
    att = jax.ShapeDtypeStruct((T, D_ATT), MXU_DTYPE)
    out_shape = [att] * 7 + [
        jax.ShapeDtypeStruct((T, IDX_DIM), MXU_DTYPE),
        jax.ShapeDtypeStruct((T, N_IDX_HEADS), F32),
        jax.ShapeDtypeStruct((T, D_RNN), F32),
        jax.ShapeDtypeStruct((T, D_RNN), F32),
        jax.ShapeDtypeStruct((T, 3 * D), F32),
        jax.ShapeDtypeStruct((nt, 1, D_ATT), F32),
    ]
    out_specs = [row(D_ATT)] * 7 + [row(IDX_DIM), row(N_IDX_HEADS), row(D_RNN), row(D_RNN), row(3 * D),
                                    pl.BlockSpec((1, 1, D_ATT), lambda t: (t, 0, 0))]
    return pl.pallas_call(
        _inproj_kernel,
        grid=(nt,),
        in_specs=[row(D), _resident((1, D)), _resident(w_cat.shape)],
        out_specs=out_specs,
        out_shape=out_shape,
        compiler_params=_cparams(("arbitrary",)),
        name="inproj",
    )(x2, g, w_cat)


N_NEAR = 3


def _bias_kernel(tab_ref, o_ref):
    w = pl.program_id(0)
    h = pl.program_id(1)
    key = lax.broadcasted_iota(jnp.int32, (TILE, Q_TILE), 0)
    qry = lax.broadcasted_iota(jnp.int32, (TILE, Q_TILE), 1)
    dist = TILE * (w - 1) + qry - key
    n = jnp.maximum(dist, 0)
    max_exact = N_BUCKETS // 2
    nf = jnp.maximum(n, 1).astype(F32)
    large = max_exact + (jnp.log(nf / max_exact) / math.log(MAX_DISTANCE / max_exact)
                         * (N_BUCKETS - max_exact)).astype(jnp.int32)
    large = jnp.minimum(large, N_BUCKETS - 1)
    bucket = jnp.where(n < max_exact, n, large)
    out = jnp.zeros((TILE, Q_TILE), F32)
    for b in range(N_BUCKETS):
        out = jnp.where(bucket == b, tab_ref[b, h], out)
    o_ref[0, 0] = jnp.where(dist >= 0, out - tab_ref[N_BUCKETS - 1, h], NEG)


def _bias_tiles(rel_bias):
    nh = rel_bias.shape[1]
    return pl.pallas_call(
        _bias_kernel,
        grid=(N_NEAR, nh),
        in_specs=[pl.BlockSpec(memory_space=pltpu.SMEM)],
        out_specs=pl.BlockSpec((1, 1, TILE, Q_TILE), lambda w, h: (w, h, 0, 0)),
        out_shape=jax.ShapeDtypeStruct((N_NEAR, nh, TILE, Q_TILE), F32),
        compiler_params=_cparams(("arbitrary", "arbitrary")),
        name="t5_bias_tiles",
    )(rel_bias)


def _moba_sel_kernel(q_ref, k_ref, km_ref, qx_ref, kx_ref):
    i = pl.program_id(1)
    q = q_ref[0]
    k = k_ref[0]
    km = km_ref[0].astype(MXU_DTYPE)
    n_iota = lax.broadcasted_iota(jnp.int32, (TILE, MAX_BLOCKS), 1)
    past = n_iota < i
    past_bias = jnp.where(n_iota <= i, 0.0, NEG)
    onehot = jnp.where(n_iota == i, 1.0, 0.0).astype(kx_ref.dtype)
    for h in range(N_HEADS):
        hs = slice(h * HEAD_DIM, (h + 1) * HEAD_DIM)
        qh = q[:, hs]
        gate = _dot_nt(qh, km[:, hs]) * (1.0 / ATT_SCALE)
        gate = jnp.where(past, gate, NEG)
        choice_bias = jnp.where(n_iota == i, 0.0, NEG)
        for _ in range(MOBA_TOPK):
            m = jnp.max(gate, axis=1, keepdims=True)
            idx = jnp.min(jnp.where(gate == m, n_iota, MAX_BLOCKS), axis=1, keepdims=True)
            hit = n_iota == idx
            choice_bias = jnp.where(hit, past_bias, choice_bias)
            gate = jnp.where(hit, -jnp.inf, gate)
        lo = h * 2 * HEAD_DIM
        qx_ref[0, :, lo:lo + HEAD_DIM] = qh
        qx_ref[0, :, lo + HEAD_DIM:lo + 2 * HEAD_DIM] = choice_bias.astype(qx_ref.dtype)
        kx_ref[0, :, lo:lo + HEAD_DIM] = k[:, hs]
        kx_ref[0, :, lo + HEAD_DIM:lo + 2 * HEAD_DIM] = onehot


def _moba_sel(q, k, kmean_pad):
    B, S, _ = q.shape
    tile = pl.BlockSpec((1, TILE, D_ATT), lambda b, i: (b, i, 0))
    ext = pl.BlockSpec((1, TILE, 2 * D_ATT), lambda b, i: (b, i, 0))
    shp = jax.ShapeDtypeStruct((B, S, 2 * D_ATT), q.dtype)
    return pl.pallas_call(
        _moba_sel_kernel,
        grid=(B, S // TILE),
        in_specs=[tile, tile, pl.BlockSpec((1, MAX_BLOCKS, D_ATT), lambda b, i: (b, 0, 0))],
        out_specs=[ext, ext],
        out_shape=[shp, shp],
        compiler_params=_cparams(("arbitrary", "arbitrary")),
        name="moba_select",
    )(q, k, kmean_pad)


def _pair_tables(n_q_tiles):
    per = Q_TILE // TILE
    ii, jj = [], []
    for i in range(n_q_tiles):
        for j in range(per * (i + 1)):
            ii.append(i)
            jj.append(j)
    return jnp.asarray(ii, jnp.int32), jnp.asarray(jj, jnp.int32)


def _attn_kernel(it, jt, qt_ref, k_ref, vt_ref, *rest, use_mask):
    if use_mask:
        mask_ref, eye_ref, bias_ref, o_ref, m_sc, acc_sc = rest
    else:
        bias_ref, o_ref, m_sc, acc_sc = rest
    p = pl.program_id(1)
    i = it[p]
    j = jt[p]
    per = Q_TILE // TILE
    lead = per * i - j

    @pl.when(j == 0)
    def _():
        m_sc[...] = jnp.full(m_sc.shape, M_INIT, F32)
        acc_sc[...] = jnp.zeros(acc_sc.shape, F32)

    def heads(near):
        def scores(h):
            s = _dot(k_ref[0, :, h * HEAD_LANES:(h + 1) * HEAD_LANES], qt_ref[0, h])
            if use_mask:
                s = s + _dot(eye_ref[...], mask_ref[0])
            return s

        s_next = scores(0)
        for h in range(N_HEADS):
            s = s_next
            if h + 1 < N_HEADS:
                s_next = scores(h + 1)
            if near:
                s = s + bias_ref[lead + 1, h]
            m_prev = m_sc[h]
            m_new = jnp.maximum(m_prev, jnp.max(s, axis=0, keepdims=True))
            alpha = jnp.exp(m_prev - m_new)
            prob = jnp.exp(s - m_new).astype(vt_ref.dtype)
            acc_sc[h] = alpha * acc_sc[h] + _dot(vt_ref[0, h], prob)
            m_sc[h] = m_new

    @pl.when(lead >= N_NEAR - 1)
    def _():
        heads(False)

    @pl.when(lead < N_NEAR - 1)
    def _():
        heads(True)

    @pl.when(j == per * (i + 1) - 1)
    def _():
        for h in range(N_HEADS):
            acc = acc_sc[h]
            o_ref[0, h] = (acc[:HEAD_DIM] / acc[HEAD_DIM:HEAD_DIM + 1]).astype(o_ref.dtype)


def _attention(qt, k, vt, mask_t, bias_tiles):
    B, _, _, S = qt.shape
    it, jt = _pair_tables(S // Q_TILE)
    use_mask = mask_t is not None
    in_specs = [pl.BlockSpec((1, N_HEADS, HEAD_LANES, Q_TILE), lambda b, p, it, jt: (b, 0, 0, it[p])),
                pl.BlockSpec((1, TILE, N_HEADS * HEAD_LANES), lambda b, p, it, jt: (b, jt[p], 0)),
                pl.BlockSpec((1, N_HEADS, V_ROWS, TILE), lambda b, p, it, jt: (b, 0, 0, jt[p]))]
    args = [qt, k, vt]
    scratch = [pltpu.VMEM((N_HEADS, 1, Q_TILE), F32),
               pltpu.VMEM((N_HEADS, V_ROWS, Q_TILE), F32)]
    if use_mask:
        in_specs.append(pl.BlockSpec((1, TILE, Q_TILE), lambda b, p, it, jt: (b, jt[p], it[p])))
        args.append(mask_t)
        in_specs.append(pl.BlockSpec((TILE, TILE), lambda b, p, it, jt: (0, 0)))
        args.append(jnp.eye(TILE, dtype=mask_t.dtype))
    in_specs.append(pl.BlockSpec(bias_tiles.shape, lambda b, p, it, jt: (0, 0, 0, 0), pipeline_mode=pl.Buffered(1)))
    args.append(bias_tiles)
    return pl.pallas_call(
        functools.partial(_attn_kernel, use_mask=use_mask),
        grid_spec=pltpu.PrefetchScalarGridSpec(
            num_scalar_prefetch=2,
            grid=(B, it.shape[0]),
            in_specs=in_specs,
            out_specs=pl.BlockSpec((1, N_HEADS, HEAD_DIM, Q_TILE), lambda b, p, it, jt: (b, 0, 0, it[p])),
            scratch_shapes=scratch,
        ),
        out_shape=jax.ShapeDtypeStruct((B, N_HEADS, HEAD_DIM, S), vt.dtype),
        compiler_params=_cparams(("arbitrary", "arbitrary")),
        name="dsa_attention" if use_mask else "moba_attention",
    )(it, jt, *args)


SCORE_ROWS = 512
COUNT_ROWS = 512
GROUP_ROWS = 256


def _order_key(score):
    bits = pltpu.bitcast(score, jnp.int32)
    return jnp.where(bits < 0, bits ^ 0x7FFFFFFF, bits)


def _dsa_sel_kernel(qit_ref, wit_ref, ki_ref, mask_ref, keys, gmax, lim_sc, *, n_keep, seq):
    i = pl.program_id(1)
    n_count = (i * TILE + TILE + COUNT_ROWS - 1) // COUNT_ROWS
    q_pos = i * TILE + lax.broadcasted_iota(jnp.int32, (1, TILE), 1)
    w = wit_ref[0]
    keep = float(n_keep)

    gmax[...] = jnp.full(gmax.shape, NEG, F32)

    def score_chunk(c, carry):
        c0 = pl.multiple_of(c * SCORE_ROWS, SCORE_ROWS)
        kc = ki_ref[0, pl.ds(c0, SCORE_ROWS), :]
        acc = jnp.zeros((SCORE_ROWS, TILE), F32)
        for h in range(N_IDX_HEADS):
            acc = acc + w[h:h + 1, :] * jnp.maximum(_dot(kc, qit_ref[0, h]), 0.0)
        score = acc * IDX_SCALE
        score = jnp.where(score == 0.0, 0.0, score)
        key_pos = c0 + lax.broadcasted_iota(jnp.int32, (SCORE_ROWS, TILE), 0)
        score = jnp.where(key_pos <= q_pos, score, NEG)
        keys[pl.ds(c0, SCORE_ROWS), :] = _order_key(score)
        for g0 in range(0, SCORE_ROWS, GROUP_ROWS):
            gmax[...] = jnp.maximum(gmax[...], score[g0:g0 + GROUP_ROWS, :])
        return carry

    lax.fori_loop(0, n_count * (COUNT_ROWS // SCORE_ROWS), score_chunk, 0)

    def count(ones_fn):
        def body(c, acc):
            c0 = pl.multiple_of(c * COUNT_ROWS, COUNT_ROWS)
            key_pos = c0 + lax.broadcasted_iota(jnp.int32, (COUNT_ROWS, TILE), 0)
            ones = ones_fn(keys[pl.ds(c0, COUNT_ROWS), :], key_pos)
            return acc + jnp.sum(ones.reshape(COUNT_ROWS // 8, 8, TILE), axis=0)
        acc = lax.fori_loop(0, n_count, body, jnp.zeros((8, TILE), jnp.int32))
        return jnp.sum(acc.astype(F32), axis=0, keepdims=True)

    groups = gmax[...]
    lo_u = _order_key(jnp.min(groups, axis=0, keepdims=True)) ^ INT_MIN
    hi_u = _order_key(jnp.max(groups, axis=0, keepdims=True)) ^ INT_MIN
    n_bits = jnp.max(32 - lax.clz(lo_u ^ hi_u))
    low_mask = jnp.where(n_bits >= 32, jnp.int32(-1), jnp.left_shift(jnp.int32(1), n_bits) - 1)

    def bit_step(b, thr_u):
        cand_u = thr_u | jnp.left_shift(jnp.int32(1), n_bits - 1 - b)
        cand_s = cand_u ^ INT_MIN
        cnt = count(lambda kk, pos: jnp.where(kk >= cand_s, 1, 0))
        return jnp.where(cnt >= keep, cand_u, thr_u)

    thr = lax.fori_loop(0, n_bits, bit_step, lo_u & ~low_mask) ^ INT_MIN
    cnt_ge = count(lambda kk, pos: jnp.where(kk >= thr, 1, 0))
    cnt_gt = count(lambda kk, pos: jnp.where(kk > thr, 1, 0))
    need = keep - cnt_gt
    crowded = cnt_ge > keep
    any_crowded = jnp.max(cnt_ge) > keep

    def write(chosen_fn):
        def body(c, carry):
            c0 = pl.multiple_of(c * COUNT_ROWS, COUNT_ROWS)
            key_pos = c0 + lax.broadcasted_iota(jnp.int32, (COUNT_ROWS, TILE), 0)
            bias = chosen_fn(keys[pl.ds(c0, COUNT_ROWS), :], key_pos)
            mask_ref[0, pl.ds(c0, COUNT_ROWS), :] = jnp.where(key_pos <= q_pos, bias, NEG).astype(mask_ref.dtype)
            return carry
        lax.fori_loop(0, n_count, body, 0)

    @pl.when(jnp.logical_not(any_crowded))
    def _():
        write(lambda kk, pos: jnp.where(kk >= thr, 0.0, NEG))

    @pl.when(any_crowded)
    def _():
        n_idx_bits = int(math.log2(seq))

        def idx_step(b, lim):
            cand = lim | jnp.left_shift(jnp.int32(1), n_idx_bits - 1 - b)
            cnt = count(lambda kk, pos: jnp.where(kk == thr, jnp.where(pos < cand, 1, 0), 0))
            return jnp.where(cnt < need, cand, lim)

        lim = lax.fori_loop(0, n_idx_bits, idx_step, jnp.zeros((1, TILE), jnp.int32))
        lim_sc[...] = jnp.where(crowded, lim, seq)
        lim = lim_sc[...]
        write(lambda kk, pos: jnp.where(kk > thr, 0.0, jnp.where(kk == thr, jnp.where(pos <= lim, 0.0, NEG), NEG)))

    def fill_chunk(c, carry):
        c0 = pl.multiple_of(c * COUNT_ROWS, COUNT_ROWS)
        mask_ref[0, pl.ds(c0, COUNT_ROWS), :] = jnp.full((COUNT_ROWS, TILE), NEG, mask_ref.dtype)
        return carry

    lax.fori_loop(n_count, seq // COUNT_ROWS, fill_chunk, 0)


def _dsa_sel(qi_t, wi_t, ki, n_keep):
    B, S, _ = ki.shape
    assert n_keep <= GROUP_ROWS and S % COUNT_ROWS == 0
    return pl.pallas_call(
        functools.partial(_dsa_sel_kernel, n_keep=n_keep, seq=S),
        grid=(B, S // TILE),
        in_specs=[pl.BlockSpec((1, N_IDX_HEADS, IDX_DIM, TILE), lambda b, i: (b, 0, 0, i)),
                  pl.BlockSpec((1, N_IDX_HEADS, TILE), lambda b, i: (b, 0, i)),
                  pl.BlockSpec((1, S, IDX_DIM), lambda b, i: (b, 0, 0))],
        out_specs=pl.BlockSpec((1, S, TILE), lambda b, i: (b, 0, i)),
        out_shape=jax.ShapeDtypeStruct((B, S, S), jnp.bfloat16),
        scratch_shapes=[pltpu.VMEM((S, TILE), jnp.int32),
                        pltpu.VMEM((GROUP_ROWS, TILE), F32),
                        pltpu.VMEM((1, TILE), jnp.int32)],
        compiler_params=_cparams(("arbitrary", "arbitrary")),
        name="dsa_select",
    )(qi_t, wi_t, ki)


def _gelu_tanh(x):
    return 0.5 * x * (1.0 + jnp.tanh(math.sqrt(2.0 / math.pi) * (x + 0.044715 * (x * x * x))))


def _rnn_kernel(xc_ref, yc_ref, cw_ref, cb_ref, wr_ref, br_ref, wi_ref, bi_ref, lam_ref, o_ref, xbuf, h_sc):
    L = xc_ref.shape[1]
    pad = 8

    @pl.when(pl.program_id(1) == 0)
    def _():
        xbuf[0:pad, :] = jnp.zeros((pad, D_RNN), F32)
        h_sc[...] = jnp.zeros(h_sc.shape, F32)

    x = xc_ref[0]
    xbuf[pad:pad + L, :] = x
    xconv = cb_ref[...]
    for tap in range(CONV_WIDTH):
        back = CONV_WIDTH - 1 - tap
        xconv = xconv + cw_ref[tap:tap + 1, :] * xbuf[pad - back:pad - back + L, :]
    xbuf[0:pad, :] = xbuf[L:L + pad, :]

    xb = xconv.astype(MXU_DTYPE)
    r = jax.nn.sigmoid(_dot(xb, wr_ref[...]) + br_ref[...])
    gi = jax.nn.sigmoid(_dot(xb, wi_ref[...]) + bi_ref[...])
    neg_lam = -lam_ref[...]
    softplus = jnp.maximum(neg_lam, 0.0) + jnp.log1p(jnp.exp(-jnp.abs(neg_lam)))
    log_a = -LRU_C * r * softplus
    a = jnp.exp(log_a)
    u = jnp.sqrt(-jnp.tanh(log_a) * (a * a + 1.0)) * (gi * xconv)

    row = lax.broadcasted_iota(jnp.int32, (L, D_RNN), 0)
    d = 1
    while d < L:
        keep = row >= d
        u = u + a * jnp.where(keep, pltpu.roll(u, d, 0), 0.0)
        a = a * jnp.where(keep, pltpu.roll(a, d, 0), 1.0)
        d *= 2
    hseq = u + a * h_sc[...]
    h_sc[...] = hseq[L - 1:L, :]
    o_ref[0] = (hseq * _gelu_tanh(yc_ref[0])).astype(o_ref.dtype)


def _rnn(xc, yc, conv_w, conv_b, wr_bd, b_r, wi_bd, b_i, lam):
    B, S, _ = xc.shape
    L = TILE
    tile = pl.BlockSpec((1, L, D_RNN), lambda b, s: (b, s, 0))
    vec = _resident((1, D_RNN))
    return pl.pallas_call(
        _rnn_kernel,
        grid=(B, S // L),
        in_specs=[tile, tile, _resident((CONV_WIDTH, D_RNN)), vec, _resident((D_RNN, D_RNN)), vec,
                  _resident((D_RNN, D_RNN)), vec, vec],
        out_specs=tile,
        out_shape=jax.ShapeDtypeStruct((B, S, D_RNN), MXU_DTYPE),
        scratch_shapes=[pltpu.VMEM((L + 8, D_RNN), F32), pltpu.VMEM((1, D_RNN), F32)],
        compiler_params=_cparams(("arbitrary", "arbitrary")),
        name="conv_rglru",
    )(xc, yc, conv_w, conv_b, wr_bd, b_r, wi_bd, b_i, lam)


def _post_kernel(x_ref, ya_ref, yb_ref, yc_ref, gt_ref, wpa, wpb, wpc, wo, g_mlp, w_up, w_down, g_fin, o_ref,
                 *, final):
    D = x_ref.shape[1]
    merged = jax.nn.sigmoid(gt_ref[:, 0:D]) * _dot(ya_ref[...], wpa[...])
    merged = merged + jax.nn.sigmoid(gt_ref[:, D:2 * D]) * _dot(yb_ref[...], wpb[...])
    merged = merged + jax.nn.sigmoid(gt_ref[:, 2 * D:3 * D]) * _dot(yc_ref[...], wpc[...])
    x1 = x_ref[...] + _dot(merged.astype(MXU_DTYPE), wo[...])
    xn = _rms(x1, g_mlp[...]).astype(MXU_DTYPE)
    d_ff = w_up.shape[1]
    acc = x1
    for c in range(d_ff // 1024):
        hid = jnp.maximum(_dot(xn, w_up[:, c * 1024:(c + 1) * 1024]), 0.0)
        acc = acc + _dot((hid * hid).astype(MXU_DTYPE), w_down[c * 1024:(c + 1) * 1024, :])
    if final:
        acc = _rms(acc, g_fin[...])
    o_ref[...] = acc


def _post(x2, ya, yb, yc, gates, wpa, wpb, wpc, wo, g_mlp, w_up, w_down, g_fin, final):
    T, D = x2.shape
    row = lambda w: pl.BlockSpec((TILE, w), lambda t: (t, 0))
    weights = (wpa, wpb, wpc, wo, g_mlp, w_up, w_down, g_fin)
    return pl.pallas_call(
        functools.partial(_post_kernel, final=final),
        grid=(T // TILE,),
        in_specs=[row(D), row(D_ATT), row(D_ATT), row(D_RNN), row(3 * D)] + [_resident(w.shape) for w in weights],
        out_specs=row(D),
        out_shape=jax.ShapeDtypeStruct((T, D), F32),
        compiler_params=_cparams(("arbitrary",)),
        name="merge_mlp",
    )(x2, ya, yb, yc, gates, *weights)


def _block_diag(w):
    return jax.scipy.linalg.block_diag(*[w[n] for n in range(w.shape[0])])


def kernel(x, rel_bias, norm_mix_g, w_in, conv_w, conv_b, w_r, b_r, w_i, b_i, lru_lambda, w_pa, w_pb, w_pc, w_o,
           norm_mlp_g, w_up, w_down, final_norm_g):
    B, S, D = x.shape
    depth = w_in.shape[0]
    nb = S // TILE
    assert S % Q_TILE == 0 and nb <= MAX_BLOCKS and S & (S - 1) == 0
    n_keep = min(DSA_MAX_TOPK, S // 4)
    cast = lambda a: a.astype(MXU_DTYPE)
    vec = lambda a: a.reshape(1, -1)

    bias_tiles = _bias_tiles(rel_bias)
    x2 = x.reshape(B * S, D)
    seq = lambda a: a.reshape(B, S, a.shape[-1])
    heads = lambda a: a.reshape(B, S, N_HEADS, a.shape[-1] // N_HEADS)
    heads_t = lambda a: jnp.transpose(heads(a), (0, 2, 3, 1))
    ones_rows = jnp.ones((B, N_HEADS, V_ROWS - HEAD_DIM, S), MXU_DTYPE)
    values_t = lambda a: jnp.concatenate([heads_t(a), ones_rows], axis=2)
    pad_heads = lambda a: jnp.pad(heads(a), ((0, 0), (0, 0), (0, 0), (0, HEAD_LANES - HEAD_DIM)))
    tokens = lambda yt: jnp.transpose(yt, (0, 3, 1, 2)).reshape(B * S, D_ATT)
    for l in range(depth):
        w = w_in[l]
        n_small = IDX_DIM + N_IDX_HEADS
        w_cat = cast(jnp.concatenate(
            [w[:, :N_ATT_COLS + n_small], jnp.zeros((D, SMALL_COLS - n_small), w.dtype),
             w[:, N_ATT_COLS + n_small:]], axis=1))
        (qa, ka, va, qb, kb, vb, qi, ki, wi, xc, yc, gates, kmean) = _inproj(x2, vec(norm_mix_g[l]), w_cat)

        kmean_pad = jnp.pad(kmean.reshape(B, nb, D_ATT), ((0, 0), (0, MAX_BLOCKS - nb), (0, 0)))
        qx, kx = _moba_sel(seq(qa), seq(ka), kmean_pad)
        y_a = _attention(heads_t(qx), kx, values_t(va), None, bias_tiles[:, :N_HEADS])

        mask_t = _dsa_sel(heads_t(qi), jnp.swapaxes(seq(wi), 1, 2), seq(ki), n_keep)
        y_b = _attention(jnp.transpose(pad_heads(qb), (0, 2, 3, 1)), pad_heads(kb).reshape(B, S, -1), values_t(vb),
                         mask_t, bias_tiles[:, N_HEADS:])

        y_c = _rnn(seq(xc), seq(yc), conv_w[l], vec(conv_b[l]), cast(_block_diag(w_r[l])), vec(b_r[l]),
                   cast(_block_diag(w_i[l])), vec(b_i[l]), vec(lru_lambda[l]))

        x2 = _post(x2, tokens(y_a), tokens(y_b), y_c.reshape(B * S, D_RNN), gates,
                   cast(w_pa[l]), cast(w_pb[l]), cast(w_pc[l]), cast(w_o[l]), vec(norm_mlp_g[l]),
                   cast(w_up[l]), cast(w_down[l]), vec(final_norm_g), final=(l == depth - 1))
    return x2.reshape(B, S, D)
```

```python
import functools
import math

import numpy as np
import jax
import jax.numpy as jnp
from jax import lax
from jax.experimental import pallas as pl
from jax.experimental.pallas import tpu as pltpu

F32 = jnp.float32
MXU_DTYPE = jnp.bfloat16

HEAD_DIM = 64
N_HEADS = 8
D_ATT = N_HEADS * HEAD_DIM
IDX_DIM = 64
N_IDX_HEADS = 8
D_RNN = 512
CONV_WIDTH = 4
LRU_C = 8.0
MOBA_TOPK = 3
DSA_MAX_TOPK = 256
N_BUCKETS = 32
MAX_DISTANCE = 128
EPS = 1e-6
NEG = -1e30
M_INIT = -1e29
ATT_SCALE = HEAD_DIM ** -0.5
IDX_SCALE = (N_IDX_HEADS ** -0.5) * (IDX_DIM ** -0.5)

TILE = 256
Q_TILE = 2 * TILE
HEAD_LANES = 2 * HEAD_DIM
V_ROWS = HEAD_DIM + 16
MAX_BLOCKS = 64
VMEM_LIMIT = 56 * 1024 * 1024
INT_MIN = -2 ** 31


def _cparams(sem):
    return pltpu.CompilerParams(dimension_semantics=sem, vmem_limit_bytes=VMEM_LIMIT)


def _resident(shape):
    nd = len(shape)
    return pl.BlockSpec(shape, lambda *_: (0,) * nd, pipeline_mode=pl.Buffered(1))


def _rms(x, g):
    return x * lax.rsqrt(jnp.mean(x * x, axis=-1, keepdims=True) + EPS) * g


def _dot(a, b):
    return jnp.dot(a, b, preferred_element_type=F32)


def _dot_nt(a, b):
    return lax.dot_general(a, b, (((1,), (1,)), ((), ())), preferred_element_type=F32)


N_ATT_COLS = 7 * D_ATT
SMALL_COLS = 128
W_CAT_COLS = N_ATT_COLS + SMALL_COLS + 2 * D_RNN + 3 * 1024


def _inproj_kernel(x_ref, g_ref, w_ref, qa, ka, va, qb, kb, vb, qi, ki, wi, xc, yc, gt, kmean):
    xn = _rms(x_ref[...], g_ref[...]).astype(MXU_DTYPE)

    def mm(lo, width):
        return _dot(xn, w_ref[:, lo:lo + width])

    for n, o in enumerate((qa, ka, va, qb, kb, vb, qi)):
        r = mm(n * D_ATT, D_ATT)
        if n in (0, 3):
            r = r * ATT_SCALE
        if n == 1:
            kmean[0] = jnp.mean(r, axis=0, keepdims=True)
        o[...] = r.astype(o.dtype)
    r = mm(N_ATT_COLS, SMALL_COLS)
    ki[...] = r[:, :IDX_DIM].astype(ki.dtype)
    wi[...] = r[:, IDX_DIM:IDX_DIM + N_IDX_HEADS]
    lo = N_ATT_COLS + SMALL_COLS
    xc[...] = mm(lo, D_RNN)
    yc[...] = mm(lo + D_RNN, D_RNN)
    lo += 2 * D_RNN
    for c in range(3):
        gt[:, c * 1024:(c + 1) * 1024] = mm(lo + c * 1024, 1024)


def _inproj(x2, g, w_cat):
    T, D = x2.shape
    nt = T // TILE
    row = lambda w: pl.BlockSpec((TILE, w), lambda t: (t, 0))
    att = jax.ShapeDtypeStruct((T, D_ATT), MXU_DTYPE)
    out_shape = [att] * 7 + [
        jax.ShapeDtypeStruct((T, IDX_DIM), MXU_DTYPE),
        jax.ShapeDtypeStruct((T, N_IDX_HEADS), F32),
        jax.ShapeDtypeStruct((T, D_RNN), F32),
        jax.ShapeDtypeStruct((T, D_RNN), F32),
        jax.ShapeDtypeStruct((T, 3 * D), F32),
        jax.ShapeDtypeStruct((nt, 1, D_ATT), F32),
    ]
    out_specs = [row(D_ATT)] * 7 + [row(IDX_DIM), row(N_IDX_HEADS), row(D_RNN), row(D_RNN), row(3 * D),
                                    pl.BlockSpec((1, 1, D_ATT), lambda t: (t, 0, 0))]
    return pl.pallas_call(
        _inproj_kernel,
        grid=(nt,),
        in_specs=[row(D), _resident((1, D)), _resident(w_cat.shape)],
        out_specs=out_specs,
        out_shape=out_shape,
        compiler_params=_cparams(("arbitrary",)),
        name="inproj",
    )(x2, g, w_cat)


N_NEAR = 3


def _bias_kernel(tab_ref, o_ref):
    w = pl.program_id(0)
    h = pl.program_id(1)
    key = lax.broadcasted_iota(jnp.int32, (TILE, Q_TILE), 0)
    qry = lax.broadcasted_iota(jnp.int32, (TILE, Q_TILE), 1)
    dist = TILE * (w - 1) + qry - key
    n = jnp.maximum(dist, 0)
    max_exact = N_BUCKETS // 2
    nf = jnp.maximum(n, 1).astype(F32)
    large = max_exact + (jnp.log(nf / max_exact) / math.log(MAX_DISTANCE / max_exact)
                         * (N_BUCKETS - max_exact)).astype(jnp.int32)
    large = jnp.minimum(large, N_BUCKETS - 1)
    bucket = jnp.where(n < max_exact, n, large)
    out = jnp.zeros((TILE, Q_TILE), F32)
    for b in range(N_BUCKETS):
        out = jnp.where(bucket == b, tab_ref[b, h], out)
    o_ref[0, 0] = jnp.where(dist >= 0, out - tab_ref[N_BUCKETS - 1, h], NEG)


def _bias_tiles(rel_bias):
    nh = rel_bias.shape[1]
    return pl.pallas_call(
        _bias_kernel,
        grid=(N_NEAR, nh),
        in_specs=[pl.BlockSpec(memory_space=pltpu.SMEM)],
        out_specs=pl.BlockSpec((1, 1, TILE, Q_TILE), lambda w, h: (w, h, 0, 0)),
        out_shape=jax.ShapeDtypeStruct((N_NEAR, nh, TILE, Q_TILE), F32),
        compiler_params=_cparams(("arbitrary", "arbitrary")),
        name="t5_bias_tiles",
    )(rel_bias)


def _moba_sel_kernel(q_ref, k_ref, km_ref, qx_ref, kx_ref):
    i = pl.program_id(1)
    q = q_ref[0]
    k = k_ref[0]
    km = km_ref[0].astype(MXU_DTYPE)
    n_iota = lax.broadcasted_iota(jnp.int32, (TILE, MAX_BLOCKS), 1)
    past = n_iota < i
    past_bias = jnp.where(n_iota <= i, 0.0, NEG)
    onehot = jnp.where(n_iota == i, 1.0, 0.0).astype(kx_ref.dtype)
    for h in range(N_HEADS):
        hs = slice(h * HEAD_DIM, (h + 1) * HEAD_DIM)
        qh = q[:, hs]
        gate = _dot_nt(qh, km[:, hs]) * (1.0 / ATT_SCALE)
        gate = jnp.where(past, gate, NEG)
        choice_bias = jnp.where(n_iota == i, 0.0, NEG)
        for _ in range(MOBA_TOPK):
            m = jnp.max(gate, axis=1, keepdims=True)
            idx = jnp.min(jnp.where(gate == m, n_iota, MAX_BLOCKS), axis=1, keepdims=True)
            hit = n_iota == idx
            choice_bias = jnp.where(hit, past_bias, choice_bias)
            gate = jnp.where(hit, -jnp.inf, gate)
        lo = h * 2 * HEAD_DIM
        qx_ref[0, :, lo:lo + HEAD_DIM] = qh
        qx_ref[0, :, lo + HEAD_DIM:lo + 2 * HEAD_DIM] = choice_bias.astype(qx_ref.dtype)
        kx_ref[0, :, lo:lo + HEAD_DIM] = k[:, hs]
        kx_ref[0, :, lo + HEAD_DIM:lo + 2 * HEAD_DIM] = onehot


def _moba_sel(q, k, kmean_pad):
    B, S, _ = q.shape
    tile = pl.BlockSpec((1, TILE, D_ATT), lambda b, i: (b, i, 0))
    ext = pl.BlockSpec((1, TILE, 2 * D_ATT), lambda b, i: (b, i, 0))
    shp = jax.ShapeDtypeStruct((B, S, 2 * D_ATT), q.dtype)
    return pl.pallas_call(
        _moba_sel_kernel,
        grid=(B, S // TILE),
        in_specs=[tile, tile, pl.BlockSpec((1, MAX_BLOCKS, D_ATT), lambda b, i: (b, 0, 0))],
        out_specs=[ext, ext],
        out_shape=[shp, shp],
        compiler_params=_cparams(("arbitrary", "arbitrary")),
        name="moba_select",
    )(q, k, kmean_pad)


def _pair_tables(n_q_tiles):
    per = Q_TILE // TILE
    ii, jj = [], []
    for i in range(n_q_tiles):
        for j in range(per * (i + 1)):
            ii.append(i)
            jj.append(j)
    return jnp.asarray(ii, jnp.int32), jnp.asarray(jj, jnp.int32)


def _attn_kernel(it, jt, qt_ref, k_ref, vt_ref, *rest, use_mask):
    if use_mask:
        mask_ref, eye_ref, bias_ref, o_ref, m_sc, acc_sc = rest
    else:
        bias_ref, o_ref, m_sc, acc_sc = rest
    p = pl.program_id(1)
    i = it[p]
    j = jt[p]
    per = Q_TILE // TILE
    lead = per * i - j

    @pl.when(j == 0)
    def _():
        m_sc[...] = jnp.full(m_sc.shape, M_INIT, F32)
        acc_sc[...] = jnp.zeros(acc_sc.shape, F32)

    def heads(near):
        def scores(h):
            s = _dot(k_ref[0, :, h * HEAD_LANES:(h + 1) * HEAD_LANES], qt_ref[0, h])
            if use_mask:
                s = s + _dot(eye_ref[...], mask_ref[0])
            return s

        s_next = scores(0)
        for h in range(N_HEADS):
            s = s_next
            if h + 1 < N_HEADS:
                s_next = scores(h + 1)
            if near:
                s = s + bias_ref[lead + 1, h]
            m_prev = m_sc[h]
            m_new = jnp.maximum(m_prev, jnp.max(s, axis=0, keepdims=True))
            alpha = jnp.exp(m_prev - m_new)
            prob = jnp.exp(s - m_new).astype(vt_ref.dtype)
            acc_sc[h] = alpha * acc_sc[h] + _dot(vt_ref[0, h], prob)
            m_sc[h] = m_new

    @pl.when(lead >= N_NEAR - 1)
    def _():
        heads(False)

    @pl.when(lead < N_NEAR - 1)
    def _():
        heads(True)

    @pl.when(j == per * (i + 1) - 1)
    def _():
        for h in range(N_HEADS):
            acc = acc_sc[h]
            o_ref[0, h] = (acc[:HEAD_DIM] / acc[HEAD_DIM:HEAD_DIM + 1]).astype(o_ref.dtype)


def _attention(qt, k, vt, mask_t, bias_tiles):
    B, _, _, S = qt.shape
    it, jt = _pair_tables(S // Q_TILE)
    use_mask = mask_t is not None
    in_specs = [pl.BlockSpec((1, N_HEADS, HEAD_LANES, Q_TILE), lambda b, p, it, jt: (b, 0, 0, it[p])),
                pl.BlockSpec((1, TILE, N_HEADS * HEAD_LANES), lambda b, p, it, jt: (b, jt[p], 0)),
                pl.BlockSpec((1, N_HEADS, V_ROWS, TILE), lambda b, p, it, jt: (b, 0, 0, jt[p]))]
    args = [qt, k, vt]
    scratch = [pltpu.VMEM((N_HEADS, 1, Q_TILE), F32),
               pltpu.VMEM((N_HEADS, V_ROWS, Q_TILE), F32)]
    if use_mask:
        in_specs.append(pl.BlockSpec((1, TILE, Q_TILE), lambda b, p, it, jt: (b, jt[p], it[p])))
        args.append(mask_t)
        in_specs.append(pl.BlockSpec((TILE, TILE), lambda b, p, it, jt: (0, 0)))
        args.append(jnp.eye(TILE, dtype=mask_t.dtype))
    in_specs.append(pl.BlockSpec(bias_tiles.shape, lambda b, p, it, jt: (0, 0, 0, 0), pipeline_mode=pl.Buffered(1)))
    args.append(bias_tiles)
    return pl.pallas_call(
        functools.partial(_attn_kernel, use_mask=use_mask),
        grid_spec=pltpu.PrefetchScalarGridSpec(
            num_scalar_prefetch=2,
            grid=(B, it.shape[0]),
            in_specs=in_specs,
            out_specs=pl.BlockSpec((1, N_HEADS, HEAD_DIM, Q_TILE), lambda b, p, it, jt: (b, 0, 0, it[p])),
            scratch_shapes=scratch,
        ),
        out_shape=jax.ShapeDtypeStruct((B, N_HEADS, HEAD_DIM, S), vt.dtype),
        compiler_params=_cparams(("arbitrary", "arbitrary")),
        name="dsa_attention" if use_mask else "moba_attention",
    )(it, jt, *args)


SCORE_ROWS = 512
COUNT_ROWS = 512
COUNT_ACC_ROWS = 32
GROUP_ROWS = 256


def _order_key(score):
    bits = pltpu.bitcast(score, jnp.int32)
    return jnp.where(bits < 0, bits ^ 0x7FFFFFFF, bits)


def _dsa_sel_kernel(qit_ref, wit_ref, ki_ref, mask_ref, keys, gmax, lim_sc, *, n_keep, seq):
    i = pl.program_id(1)
    n_count = (i * TILE + TILE + COUNT_ROWS - 1) // COUNT_ROWS
    q_pos = i * TILE + lax.broadcasted_iota(jnp.int32, (1, TILE), 1)
    w = wit_ref[0]
    keep = float(n_keep)

    gmax[...] = jnp.full(gmax.shape, NEG, F32)

    def score_chunk(c, carry):
        c0 = pl.multiple_of(c * SCORE_ROWS, SCORE_ROWS)
        kc = ki_ref[0, pl.ds(c0, SCORE_ROWS), :]
        acc = jnp.zeros((SCORE_ROWS, TILE), F32)
        for h in range(N_IDX_HEADS):
            acc = acc + w[h:h + 1, :] * jnp.maximum(_dot(kc, qit_ref[0, h]), 0.0)
        score = acc * IDX_SCALE
        score = jnp.where(score == 0.0, 0.0, score)
        key_pos = c0 + lax.broadcasted_iota(jnp.int32, (SCORE_ROWS, TILE), 0)
        score = jnp.where(key_pos <= q_pos, score, NEG)
        keys[pl.ds(c0, SCORE_ROWS), :] = _order_key(score)
        for g0 in range(0, SCORE_ROWS, GROUP_ROWS):
            gmax[...] = jnp.maximum(gmax[...], score[g0:g0 + GROUP_ROWS, :])
        return carry

    lax.fori_loop(0, n_count * (COUNT_ROWS // SCORE_ROWS), score_chunk, 0)

    def count(ones_fn):
        def body(c, acc):
            c0 = pl.multiple_of(c * COUNT_ROWS, COUNT_ROWS)
            key_pos = c0 + lax.broadcasted_iota(jnp.int32, (COUNT_ROWS, TILE), 0)
            ones = ones_fn(keys[pl.ds(c0, COUNT_ROWS), :], key_pos)
            return acc + jnp.sum(ones.reshape(COUNT_ROWS // COUNT_ACC_ROWS, COUNT_ACC_ROWS, TILE), axis=0)
        acc = lax.fori_loop(0, n_count, body, jnp.zeros((COUNT_ACC_ROWS, TILE), jnp.int32))
        return jnp.sum(acc.astype(F32), axis=0, keepdims=True)

    groups = gmax[...]
    lo_u = _order_key(jnp.min(groups, axis=0, keepdims=True)) ^ INT_MIN
    hi_u = _order_key(jnp.max(groups, axis=0, keepdims=True)) ^ INT_MIN
    n_bits = jnp.max(32 - lax.clz(lo_u ^ hi_u))
    low_mask = jnp.where(n_bits >= 32, jnp.int32(-1), jnp.left_shift(jnp.int32(1), n_bits) - 1)

    def bit_step(b, thr_u):
        cand_u = thr_u | jnp.left_shift(jnp.int32(1), n_bits - 1 - b)
        cand_s = cand_u ^ INT_MIN
        cnt = count(lambda kk, pos: jnp.where(kk >= cand_s, 1, 0))
        return jnp.where(cnt >= keep, cand_u, thr_u)

    thr = lax.fori_loop(0, n_bits, bit_step, lo_u & ~low_mask) ^ INT_MIN
    cnt_ge = count(lambda kk, pos: jnp.where(kk >= thr, 1, 0))
    cnt_gt = count(lambda kk, pos: jnp.where(kk > thr, 1, 0))
    need = keep - cnt_gt
    crowded = cnt_ge > keep
    any_crowded = jnp.max(cnt_ge) > keep

    def write(chosen_fn):
        def body(c, carry):
            c0 = pl.multiple_of(c * COUNT_ROWS, COUNT_ROWS)
            key_pos = c0 + lax.broadcasted_iota(jnp.int32, (COUNT_ROWS, TILE), 0)
            bias = chosen_fn(keys[pl.ds(c0, COUNT_ROWS), :], key_pos)
            mask_ref[0, pl.ds(c0, COUNT_ROWS), :] = jnp.where(key_pos <= q_pos, bias, NEG).astype(mask_ref.dtype)
            return carry
        lax.fori_loop(0, n_count, body, 0)

    @pl.when(jnp.logical_not(any_crowded))
    def _():
        write(lambda kk, pos: jnp.where(kk >= thr, 0.0, NEG))

    @pl.when(any_crowded)
    def _():
        n_idx_bits = int(math.log2(seq))

        def idx_step(b, lim):
            cand = lim | jnp.left_shift(jnp.int32(1), n_idx_bits - 1 - b)
            cnt = count(lambda kk, pos: jnp.where(kk == thr, jnp.where(pos < cand, 1, 0), 0))
            return jnp.where(cnt < need, cand, lim)

        lim = lax.fori_loop(0, n_idx_bits, idx_step, jnp.zeros((1, TILE), jnp.int32))
        lim_sc[...] = jnp.where(crowded, lim, seq)
        lim = lim_sc[...]
        write(lambda kk, pos: jnp.where(kk > thr, 0.0, jnp.where(kk == thr, jnp.where(pos <= lim, 0.0, NEG), NEG)))

    def fill_chunk(c, carry):
        c0 = pl.multiple_of(c * COUNT_ROWS, COUNT_ROWS)
        mask_ref[0, pl.ds(c0, COUNT_ROWS), :] = jnp.full((COUNT_ROWS, TILE), NEG, mask_ref.dtype)
        return carry

    lax.fori_loop(n_count, seq // COUNT_ROWS, fill_chunk, 0)


def _dsa_sel(qi_t, wi_t, ki, n_keep):
    B, S, _ = ki.shape
    assert n_keep <= GROUP_ROWS and S % COUNT_ROWS == 0
    return pl.pallas_call(
        functools.partial(_dsa_sel_kernel, n_keep=n_keep, seq=S),
        grid=(B, S // TILE),
        in_specs=[pl.BlockSpec((1, N_IDX_HEADS, IDX_DIM, TILE), lambda b, i: (b, 0, 0, i)),
                  pl.BlockSpec((1, N_IDX_HEADS, TILE), lambda b, i: (b, 0, i)),
                  pl.BlockSpec((1, S, IDX_DIM), lambda b, i: (b, 0, 0))],
        out_specs=pl.BlockSpec((1, S, TILE), lambda b, i: (b, 0, i)),
        out_shape=jax.ShapeDtypeStruct((B, S, S), jnp.bfloat16),
        scratch_shapes=[pltpu.VMEM((S, TILE), jnp.int32),
                        pltpu.VMEM((GROUP_ROWS, TILE), F32),
                        pltpu.VMEM((1, TILE), jnp.int32)],
        compiler_params=_cparams(("arbitrary", "arbitrary")),
        name="dsa_select",
    )(qi_t, wi_t, ki)


def _gelu_tanh(x):
    return 0.5 * x * (1.0 + jnp.tanh(math.sqrt(2.0 / math.pi) * (x + 0.044715 * (x * x * x))))


def _rnn_kernel(xc_ref, yc_ref, cw_ref, cb_ref, wr_ref, br_ref, wi_ref, bi_ref, lam_ref, o_ref, xbuf, h_sc):
    L = xc_ref.shape[1]
    pad = 8

    @pl.when(pl.program_id(1) == 0)
    def _():
        xbuf[0:pad, :] = jnp.zeros((pad, D_RNN), F32)
        h_sc[...] = jnp.zeros(h_sc.shape, F32)

    x = xc_ref[0]
    xbuf[pad:pad + L, :] = x
    xconv = cb_ref[...]
    for tap in range(CONV_WIDTH):
        back = CONV_WIDTH - 1 - tap
        xconv = xconv + cw_ref[tap:tap + 1, :] * xbuf[pad - back:pad - back + L, :]
    xbuf[0:pad, :] = xbuf[L:L + pad, :]

    xb = xconv.astype(MXU_DTYPE)
    r = jax.nn.sigmoid(_dot(xb, wr_ref[...]) + br_ref[...])
    gi = jax.nn.sigmoid(_dot(xb, wi_ref[...]) + bi_ref[...])
    neg_lam = -lam_ref[...]
    softplus = jnp.maximum(neg_lam, 0.0) + jnp.log1p(jnp.exp(-jnp.abs(neg_lam)))
    log_a = -LRU_C * r * softplus
    a = jnp.exp(log_a)
    u = jnp.sqrt(-jnp.tanh(log_a) * (a * a + 1.0)) * (gi * xconv)

    row = lax.broadcasted_iota(jnp.int32, (L, D_RNN), 0)
    d = 1
    while d < L:
        keep = row >= d
        u = u + a * jnp.where(keep, pltpu.roll(u, d, 0), 0.0)
        a = a * jnp.where(keep, pltpu.roll(a, d, 0), 1.0)
        d *= 2
    hseq = u + a * h_sc[...]
    h_sc[...] = hseq[L - 1:L, :]
    o_ref[0] = (hseq * _gelu_tanh(yc_ref[0])).astype(o_ref.dtype)


def _rnn(xc, yc, conv_w, conv_b, wr_bd, b_r, wi_bd, b_i, lam):
    B, S, _ = xc.shape
    L = TILE
    tile = pl.BlockSpec((1, L, D_RNN), lambda b, s: (b, s, 0))
    vec = _resident((1, D_RNN))
    return pl.pallas_call(
        _rnn_kernel,
        grid=(B, S // L),
        in_specs=[tile, tile, _resident((CONV_WIDTH, D_RNN)), vec, _resident((D_RNN, D_RNN)), vec,
                  _resident((D_RNN, D_RNN)), vec, vec],
        out_specs=tile,
        out_shape=jax.ShapeDtypeStruct((B, S, D_RNN), MXU_DTYPE),
        scratch_shapes=[pltpu.VMEM((L + 8, D_RNN), F32), pltpu.VMEM((1, D_RNN), F32)],
        compiler_params=_cparams(("arbitrary", "arbitrary")),
        name="conv_rglru",
    )(xc, yc, conv_w, conv_b, wr_bd, b_r, wi_bd, b_i, lam)


def _post_kernel(x_ref, ya_ref, yb_ref, yc_ref, gt_ref, wpa, wpb, wpc, wo, g_mlp, w_up, w_down, g_fin, o_ref,
                 *, final):
    D = x_ref.shape[1]
    merged = jax.nn.sigmoid(gt_ref[:, 0:D]) * _dot(ya_ref[...], wpa[...])
    merged = merged + jax.nn.sigmoid(gt_ref[:, D:2 * D]) * _dot(yb_ref[...], wpb[...])
    merged = merged + jax.nn.sigmoid(gt_ref[:, 2 * D:3 * D]) * _dot(yc_ref[...], wpc[...])
    x1 = x_ref[...] + _dot(merged.astype(MXU_DTYPE), wo[...])
    xn = _rms(x1, g_mlp[...]).astype(MXU_DTYPE)
    d_ff = w_up.shape[1]
    acc = x1
    for c in range(d_ff // 1024):
        hid = jnp.maximum(_dot(xn, w_up[:, c * 1024:(c + 1) * 1024]), 0.0)
        acc = acc + _dot((hid * hid).astype(MXU_DTYPE), w_down[c * 1024:(c + 1) * 1024, :])
    if final:
        acc = _rms(acc, g_fin[...])
    o_ref[...] = acc


def _post(x2, ya, yb, yc, gates, wpa, wpb, wpc, wo, g_mlp, w_up, w_down, g_fin, final):
    T, D = x2.shape
    row = lambda w: pl.BlockSpec((TILE, w), lambda t: (t, 0))
    weights = (wpa, wpb, wpc, wo, g_mlp, w_up, w_down, g_fin)
    return pl.pallas_call(
        functools.partial(_post_kernel, final=final),
        grid=(T // TILE,),
        in_specs=[row(D), row(D_ATT), row(D_ATT), row(D_RNN), row(3 * D)] + [_resident(w.shape) for w in weights],
        out_specs=row(D),
        out_shape=jax.ShapeDtypeStruct((T, D), F32),
        compiler_params=_cparams(("arbitrary",)),
        name="merge_mlp",
    )(x2, ya, yb, yc, gates, *weights)


def _block_diag(w):
    return jax.scipy.linalg.block_diag(*[w[n] for n in range(w.shape[0])])


def kernel(x, rel_bias, norm_mix_g, w_in, conv_w, conv_b, w_r, b_r, w_i, b_i, lru_lambda, w_pa, w_pb, w_pc, w_o,
           norm_mlp_g, w_up, w_down, final_norm_g):
    B, S, D = x.shape
    depth = w_in.shape[0]
    nb = S // TILE
    assert S % Q_TILE == 0 and nb <= MAX_BLOCKS and S & (S - 1) == 0
    n_keep = min(DSA_MAX_TOPK, S // 4)
    cast = lambda a: a.astype(MXU_DTYPE)
    vec = lambda a: a.reshape(1, -1)

    bias_tiles = _bias_tiles(rel_bias)
    x2 = x.reshape(B * S, D)
    seq = lambda a: a.reshape(B, S, a.shape[-1])
    heads = lambda a: a.reshape(B, S, N_HEADS, a.shape[-1] // N_HEADS)
    heads_t = lambda a: jnp.transpose(heads(a), (0, 2, 3, 1))
    ones_rows = jnp.ones((B, N_HEADS, V_ROWS - HEAD_DIM, S), MXU_DTYPE)
    values_t = lambda a: jnp.concatenate([heads_t(a), ones_rows], axis=2)
    pad_heads = lambda a: jnp.pad(heads(a), ((0, 0), (0, 0), (0, 0), (0, HEAD_LANES - HEAD_DIM)))
    tokens = lambda yt: jnp.transpose(yt, (0, 3, 1, 2)).reshape(B * S, D_ATT)
    for l in range(depth):
        w = w_in[l]
        n_small = IDX_DIM + N_IDX_HEADS
        w_cat = cast(jnp.concatenate(
            [w[:, :N_ATT_COLS + n_small], jnp.zeros((D, SMALL_COLS - n_small), w.dtype),
             w[:, N_ATT_COLS + n_small:]], axis=1))
        (qa, ka, va, qb, kb, vb, qi, ki, wi, xc, yc, gates, kmean) = _inproj(x2, vec(norm_mix_g[l]), w_cat)

        kmean_pad = jnp.pad(kmean.reshape(B, nb, D_ATT), ((0, 0), (0, MAX_BLOCKS - nb), (0, 0)))
        qx, kx = _moba_sel(seq(qa), seq(ka), kmean_pad)
        y_a = _attention(heads_t(qx), kx, values_t(va), None, bias_tiles[:, :N_HEADS])

        mask_t = _dsa_sel(heads_t(qi), jnp.swapaxes(seq(wi), 1, 2), seq(ki), n_keep)
        y_b = _attention(jnp.transpose(pad_heads(qb), (0, 2, 3, 1)), pad_heads(kb).reshape(B, S, -1), values_t(vb),
                         mask_t, bias_tiles[:, N_HEADS:])

        y_c = _rnn(seq(xc), seq(yc), conv_w[l], vec(conv_b[l]), cast(_block_diag(w_r[l])), vec(b_r[l]),
                   cast(_block_diag(w_i[l])), vec(b_i[l]), vec(lru_lambda[l]))

        x2 = _post(x2, tokens(y_a), tokens(y_b), y_c.reshape(B * S, D_RNN), gates,
                   cast(w_pa[l]), cast(w_pb[l]), cast(w_pc[l]), cast(w_o[l]), vec(norm_mlp_g[l]),
                   cast(w_up[l]), cast(w_down[l]), vec(final_norm_g), final=(l == depth - 1))
    return x2.reshape(B, S, D)
```

```python
import functools
import math

import numpy as np
import jax
import jax.numpy as jnp
from jax import lax
from jax.experimental import pallas as pl
from jax.experimental.pallas import tpu as pltpu

F32 = jnp.float32
MXU_DTYPE = jnp.bfloat16

HEAD_DIM = 64
N_HEADS = 8
D_ATT = N_HEADS * HEAD_DIM
IDX_DIM = 64
N_IDX_HEADS = 8
D_RNN = 512
CONV_WIDTH = 4
LRU_C = 8.0
MOBA_TOPK = 3
DSA_MAX_TOPK = 256
N_BUCKETS = 32
MAX_DISTANCE = 128
EPS = 1e-6
NEG = -1e30
M_INIT = -1e29
ATT_SCALE = HEAD_DIM ** -0.5
IDX_SCALE = (N_IDX_HEADS ** -0.5) * (IDX_DIM ** -0.5)

TILE = 256
Q_TILE = 2 * TILE
HEAD_LANES = 2 * HEAD_DIM
V_ROWS = HEAD_DIM + 16
MAX_BLOCKS = 64
VMEM_LIMIT = 56 * 1024 * 1024
INT_MIN = -2 ** 31


def _cparams(sem):
    return pltpu.CompilerParams(dimension_semantics=sem, vmem_limit_bytes=VMEM_LIMIT)


def _resident(shape):
    nd = len(shape)
    return pl.BlockSpec(shape, lambda *_: (0,) * nd, pipeline_mode=pl.Buffered(1))


def _rms(x, g):
    return x * lax.rsqrt(jnp.mean(x * x, axis=-1, keepdims=True) + EPS) * g


def _dot(a, b):
    return jnp.dot(a, b, preferred_element_type=F32)


def _dot_nt(a, b):
    return lax.dot_general(a, b, (((1,), (1,)), ((), ())), preferred_element_type=F32)


N_ATT_COLS = 7 * D_ATT
SMALL_COLS = 128
W_CAT_COLS = N_ATT_COLS + SMALL_COLS + 2 * D_RNN + 3 * 1024


def _inproj_kernel(x_ref, g_ref, w_ref, qa, ka, va, qb, kb, vb, qi, ki, wi, xc, yc, gt, kmean):
    xn = _rms(x_ref[...], g_ref[...]).astype(MXU_DTYPE)

    def mm(lo, width):
        return _dot(xn, w_ref[:, lo:lo + width])

    for n, o in enumerate((qa, ka, va, qb, kb, vb, qi)):
        r = mm(n * D_ATT, D_ATT)
        if n in (0, 3):
            r = r * ATT_SCALE
        if n == 1:
            kmean[0] = jnp.mean(r, axis=0, keepdims=True)
        o[...] = r.astype(o.dtype)
    r = mm(N_ATT_COLS, SMALL_COLS)
    ki[...] = r[:, :IDX_DIM].astype(ki.dtype)
    wi[...] = r[:, IDX_DIM:IDX_DIM + N_IDX_HEADS]
    lo = N_ATT_COLS + SMALL_COLS
    xc[...] = mm(lo, D_RNN)
    yc[...] = mm(lo + D_RNN, D_RNN)
    lo += 2 * D_RNN
    for c in range(3):
        gt[:, c * 1024:(c + 1) * 1024] = mm(lo + c * 1024, 1024)


def _inproj(x2, g, w_cat):
    T, D = x2.shape
    nt = T // TILE
    row = lambda w: pl.BlockSpec((TILE, w), lambda t: (t, 0))
    att = jax.ShapeDtypeStruct((T, D_ATT), MXU_DTYPE)
    out_shape = [att] * 7 + [
        jax.ShapeDtypeStruct((T, IDX_DIM), MXU_DTYPE),
        jax.ShapeDtypeStruct((T, N_IDX_HEADS), F32),
        jax.ShapeDtypeStruct((T, D_RNN), F32),
        jax.ShapeDtypeStruct((T, D_RNN), F32),
        jax.ShapeDtypeStruct((T, 3 * D), F32),
        jax.ShapeDtypeStruct((nt, 1, D_ATT), F32),
    ]
    out_specs = [row(D_ATT)] * 7 + [row(IDX_DIM), row(N_IDX_HEADS), row(D_RNN), row(D_RNN), row(3 * D),
                                    pl.BlockSpec((1, 1, D_ATT), lambda t: (t, 0, 0))]
    return pl.pallas_call(
        _inproj_kernel,
        grid=(nt,),
        in_specs=[row(D), _resident((1, D)), _resident(w_cat.shape)],
        out_specs=out_specs,
        out_shape=out_shape,
        compiler_params=_cparams(("arbitrary",)),
        name="inproj",
    )(x2, g, w_cat)


N_NEAR = 3


def _bias_kernel(tab_ref, o_ref):
    w = pl.program_id(0)
    h = pl.program_id(1)
    key = lax.broadcasted_iota(jnp.int32, (TILE, Q_TILE), 0)
    qry = lax.broadcasted_iota(jnp.int32, (TILE, Q_TILE), 1)
    dist = TILE * (w - 1) + qry - key
    n = jnp.maximum(dist, 0)
    max_exact = N_BUCKETS // 2
    nf = jnp.maximum(n, 1).astype(F32)
    large = max_exact + (jnp.log(nf / max_exact) / math.log(MAX_DISTANCE / max_exact)
                         * (N_BUCKETS - max_exact)).astype(jnp.int32)
    large = jnp.minimum(large, N_BUCKETS - 1)
    bucket = jnp.where(n < max_exact, n, large)
    out = jnp.zeros((TILE, Q_TILE), F32)
    for b in range(N_BUCKETS):
        out = jnp.where(bucket == b, tab_ref[b, h], out)
    o_ref[0, 0] = jnp.where(dist >= 0, out - tab_ref[N_BUCKETS - 1, h], NEG)


def _bias_tiles(rel_bias):
    nh = rel_bias.shape[1]
    return pl.pallas_call(
        _bias_kernel,
        grid=(N_NEAR, nh),
        in_specs=[pl.BlockSpec(memory_space=pltpu.SMEM)],
        out_specs=pl.BlockSpec((1, 1, TILE, Q_TILE), lambda w, h: (w, h, 0, 0)),
        out_shape=jax.ShapeDtypeStruct((N_NEAR, nh, TILE, Q_TILE), F32),
        compiler_params=_cparams(("arbitrary", "arbitrary")),
        name="t5_bias_tiles",
    )(rel_bias)


def _moba_sel_kernel(q_ref, k_ref, km_ref, qx_ref, kx_ref):
    i = pl.program_id(1)
    q = q_ref[0]
    k = k_ref[0]
    km = km_ref[0].astype(MXU_DTYPE)
    n_iota = lax.broadcasted_iota(jnp.int32, (TILE, MAX_BLOCKS), 1)
    past = n_iota < i
    past_bias = jnp.where(n_iota <= i, 0.0, NEG)
    onehot = jnp.where(n_iota == i, 1.0, 0.0).astype(kx_ref.dtype)
    for h in range(N_HEADS):
        hs = slice(h * HEAD_DIM, (h + 1) * HEAD_DIM)
        qh = q[:, hs]
        gate = _dot_nt(qh, km[:, hs]) * (1.0 / ATT_SCALE)
        gate = jnp.where(past, gate, NEG)
        choice_bias = jnp.where(n_iota == i, 0.0, NEG)
        for _ in range(MOBA_TOPK):
            m = jnp.max(gate, axis=1, keepdims=True)
            idx = jnp.min(jnp.where(gate == m, n_iota, MAX_BLOCKS), axis=1, keepdims=True)
            hit = n_iota == idx
            choice_bias = jnp.where(hit, past_bias, choice_bias)
            gate = jnp.where(hit, -jnp.inf, gate)
        lo = h * 2 * HEAD_DIM
        qx_ref[0, :, lo:lo + HEAD_DIM] = qh
        qx_ref[0, :, lo + HEAD_DIM:lo + 2 * HEAD_DIM] = choice_bias.astype(qx_ref.dtype)
        kx_ref[0, :, lo:lo + HEAD_DIM] = k[:, hs]
        kx_ref[0, :, lo + HEAD_DIM:lo + 2 * HEAD_DIM] = onehot


def _moba_sel(q, k, kmean_pad):
    B, S, _ = q.shape
    tile = pl.BlockSpec((1, TILE, D_ATT), lambda b, i: (b, i, 0))
    ext = pl.BlockSpec((1, TILE, 2 * D_ATT), lambda b, i: (b, i, 0))
    shp = jax.ShapeDtypeStruct((B, S, 2 * D_ATT), q.dtype)
    return pl.pallas_call(
        _moba_sel_kernel,
        grid=(B, S // TILE),
        in_specs=[tile, tile, pl.BlockSpec((1, MAX_BLOCKS, D_ATT), lambda b, i: (b, 0, 0))],
        out_specs=[ext, ext],
        out_shape=[shp, shp],
        compiler_params=_cparams(("arbitrary", "arbitrary")),
        name="moba_select",
    )(q, k, kmean_pad)


def _pair_tables(n_q_tiles):
    per = Q_TILE // TILE
    ii, jj = [], []
    for i in range(n_q_tiles):
        for j in range(per * (i + 1)):
            ii.append(i)
            jj.append(j)
    return jnp.asarray(ii, jnp.int32), jnp.asarray(jj, jnp.int32)


def _attn_kernel(it, jt, qt_ref, k_ref, vt_ref, *rest, use_mask):
    if use_mask:
        mask_ref, eye_ref, bias_ref, o_ref, m_sc, acc_sc = rest
    else:
        bias_ref, o_ref, m_sc, acc_sc = rest
    p = pl.program_id(1)
    i = it[p]
    j = jt[p]
    per = Q_TILE // TILE
    lead = per * i - j

    @pl.when(j == 0)
    def _():
        m_sc[...] = jnp.full(m_sc.shape, M_INIT, F32)
        acc_sc[...] = jnp.zeros(acc_sc.shape, F32)

    def heads(near):
        def scores(h):
            s = _dot(k_ref[0, :, h * HEAD_LANES:(h + 1) * HEAD_LANES], qt_ref[0, h])
            if use_mask:
                s = s + _dot(eye_ref[...], mask_ref[0])
            return s

        s_next = scores(0)
        for h in range(N_HEADS):
            s = s_next
            if h + 1 < N_HEADS:
                s_next = scores(h + 1)
            if near:
                s = s + bias_ref[lead + 1, h]
            m_prev = m_sc[h]
            m_new = jnp.maximum(m_prev, jnp.max(s, axis=0, keepdims=True))
            alpha = jnp.exp(m_prev - m_new)
            prob = jnp.exp(s - m_new).astype(vt_ref.dtype)
            acc_sc[h] = alpha * acc_sc[h] + _dot(vt_ref[0, h], prob)
            m_sc[h] = m_new

    @pl.when(lead >= N_NEAR - 1)
    def _():
        heads(False)

    @pl.when(lead < N_NEAR - 1)
    def _():
        heads(True)

    @pl.when(j == per * (i + 1) - 1)
    def _():
        for h in range(N_HEADS):
            acc = acc_sc[h]
            o_ref[0, h] = (acc[:HEAD_DIM] / acc[HEAD_DIM:HEAD_DIM + 1]).astype(o_ref.dtype)


def _attention(qt, k, vt, mask_t, bias_tiles):
    B, _, _, S = qt.shape
    it, jt = _pair_tables(S // Q_TILE)
    use_mask = mask_t is not None
    in_specs = [pl.BlockSpec((1, N_HEADS, HEAD_LANES, Q_TILE), lambda b, p, it, jt: (b, 0, 0, it[p])),
                pl.BlockSpec((1, TILE, N_HEADS * HEAD_LANES), lambda b, p, it, jt: (b, jt[p], 0)),
                pl.BlockSpec((1, N_HEADS, V_ROWS, TILE), lambda b, p, it, jt: (b, 0, 0, jt[p]))]
    args = [qt, k, vt]
    scratch = [pltpu.VMEM((N_HEADS, 1, Q_TILE), F32),
               pltpu.VMEM((N_HEADS, V_ROWS, Q_TILE), F32)]
    if use_mask:
        in_specs.append(pl.BlockSpec((1, TILE, Q_TILE), lambda b, p, it, jt: (b, jt[p], it[p])))
        args.append(mask_t)
        in_specs.append(pl.BlockSpec((TILE, TILE), lambda b, p, it, jt: (0, 0)))
        args.append(jnp.eye(TILE, dtype=mask_t.dtype))
    in_specs.append(pl.BlockSpec(bias_tiles.shape, lambda b, p, it, jt: (0, 0, 0, 0), pipeline_mode=pl.Buffered(1)))
    args.append(bias_tiles)
    return pl.pallas_call(
        functools.partial(_attn_kernel, use_mask=use_mask),
        grid_spec=pltpu.PrefetchScalarGridSpec(
            num_scalar_prefetch=2,
            grid=(B, it.shape[0]),
            in_specs=in_specs,
            out_specs=pl.BlockSpec((1, N_HEADS, HEAD_DIM, Q_TILE), lambda b, p, it, jt: (b, 0, 0, it[p])),
            scratch_shapes=scratch,
        ),
        out_shape=jax.ShapeDtypeStruct((B, N_HEADS, HEAD_DIM, S), vt.dtype),
        compiler_params=_cparams(("arbitrary", "arbitrary")),
        name="dsa_attention" if use_mask else "moba_attention",
    )(it, jt, *args)


SCORE_ROWS = 512
COUNT_ROWS = 512
COUNT_ACC_ROWS = 32
GROUP_ROWS = 256


def _order_key(score):
    bits = pltpu.bitcast(score, jnp.int32)
    return jnp.where(bits < 0, bits ^ 0x7FFFFFFF, bits)


def _dsa_sel_kernel(qit_ref, wit_ref, ki_ref, mask_ref, keys, gmax, lim_sc, *, n_keep, seq):
    i = pl.program_id(1)
    n_count = (i * TILE + TILE + COUNT_ROWS - 1) // COUNT_ROWS
    q_pos = i * TILE + lax.broadcasted_iota(jnp.int32, (1, TILE), 1)
    w = wit_ref[0]
    keep = float(n_keep)

    gmax[...] = jnp.full(gmax.shape, NEG, F32)

    def score_chunk(masked):
        def body(c, carry):
            c0 = pl.multiple_of(c * SCORE_ROWS, SCORE_ROWS)
            kc = ki_ref[0, pl.ds(c0, SCORE_ROWS), :]
            acc = jnp.zeros((SCORE_ROWS, TILE), F32)
            for h in range(N_IDX_HEADS):
                acc = acc + w[h:h + 1, :] * jnp.maximum(_dot(kc, qit_ref[0, h]), 0.0)
            score = acc * IDX_SCALE
            score = jnp.where(score == 0.0, 0.0, score)
            if masked:
                key_pos = c0 + lax.broadcasted_iota(jnp.int32, (SCORE_ROWS, TILE), 0)
                score = jnp.where(key_pos <= q_pos, score, NEG)
            keys[pl.ds(c0, SCORE_ROWS), :] = _order_key(score)
            for g0 in range(0, SCORE_ROWS, GROUP_ROWS):
                gmax[...] = jnp.maximum(gmax[...], score[g0:g0 + GROUP_ROWS, :])
            return carry
        return body

    n_score = n_count * (COUNT_ROWS // SCORE_ROWS)
    n_unmasked = jnp.minimum((i * TILE + 1) // SCORE_ROWS, n_score)
    lax.fori_loop(0, n_unmasked, score_chunk(False), 0)
    lax.fori_loop(n_unmasked, n_score, score_chunk(True), 0)

    def count(ones_fn):
        def body(c, acc):
            c0 = pl.multiple_of(c * COUNT_ROWS, COUNT_ROWS)
            key_pos = c0 + lax.broadcasted_iota(jnp.int32, (COUNT_ROWS, TILE), 0)
            ones = ones_fn(keys[pl.ds(c0, COUNT_ROWS), :], key_pos)
            return acc + jnp.sum(ones.reshape(COUNT_ROWS // COUNT_ACC_ROWS, COUNT_ACC_ROWS, TILE), axis=0)
        acc = lax.fori_loop(0, n_count, body, jnp.zeros((COUNT_ACC_ROWS, TILE), jnp.int32))
        return jnp.sum(acc.astype(F32), axis=0, keepdims=True)

    groups = gmax[...]
    lo_u = _order_key(jnp.min(groups, axis=0, keepdims=True)) ^ INT_MIN
    hi_u = _order_key(jnp.max(groups, axis=0, keepdims=True)) ^ INT_MIN
    n_bits = jnp.max(32 - lax.clz(lo_u ^ hi_u))
    low_mask = jnp.where(n_bits >= 32, jnp.int32(-1), jnp.left_shift(jnp.int32(1), n_bits) - 1)

    def bit_step(b, thr_u):
        cand_u = thr_u | jnp.left_shift(jnp.int32(1), n_bits - 1 - b)
        cand_s = cand_u ^ INT_MIN
        cnt = count(lambda kk, pos: jnp.where(kk >= cand_s, 1, 0))
        return jnp.where(cnt >= keep, cand_u, thr_u)

    thr = lax.fori_loop(0, n_bits, bit_step, lo_u & ~low_mask) ^ INT_MIN
    cnt_ge = count(lambda kk, pos: jnp.where(kk >= thr, 1, 0))
    crowded = cnt_ge > keep
    any_crowded = jnp.max(cnt_ge) > keep

    def write(chosen_fn):
        def body(c, carry):
            c0 = pl.multiple_of(c * COUNT_ROWS, COUNT_ROWS)
            key_pos = c0 + lax.broadcasted_iota(jnp.int32, (COUNT_ROWS, TILE), 0)
            bias = chosen_fn(keys[pl.ds(c0, COUNT_ROWS), :], key_pos)
            mask_ref[0, pl.ds(c0, COUNT_ROWS), :] = jnp.where(key_pos <= q_pos, bias, NEG).astype(mask_ref.dtype)
            return carry
        lax.fori_loop(0, n_count, body, 0)

    @pl.when(jnp.logical_not(any_crowded))
    def _():
        write(lambda kk, pos: jnp.where(kk >= thr, 0.0, NEG))

    @pl.when(any_crowded)
    def _():
        n_idx_bits = int(math.log2(seq))
        need = keep - count(lambda kk, pos: jnp.where(kk > thr, 1, 0))

        def idx_step(b, lim):
            cand = lim | jnp.left_shift(jnp.int32(1), n_idx_bits - 1 - b)
            cnt = count(lambda kk, pos: jnp.where(kk == thr, jnp.where(pos < cand, 1, 0), 0))
            return jnp.where(cnt < need, cand, lim)

        lim = lax.fori_loop(0, n_idx_bits, idx_step, jnp.zeros((1, TILE), jnp.int32))
        lim_sc[...] = jnp.where(crowded, lim, seq)
        lim = lim_sc[...]
        write(lambda kk, pos: jnp.where(kk > thr, 0.0, jnp.where(kk == thr, jnp.where(pos <= lim, 0.0, NEG), NEG)))

    def fill_chunk(c, carry):
        c0 = pl.multiple_of(c * COUNT_ROWS, COUNT_ROWS)
        mask_ref[0, pl.ds(c0, COUNT_ROWS), :] = jnp.full((COUNT_ROWS, TILE), NEG, mask_ref.dtype)
        return carry

    lax.fori_loop(n_count, seq // COUNT_ROWS, fill_chunk, 0)


def _dsa_sel(qi_t, wi_t, ki, n_keep):
    B, S, _ = ki.shape
    assert n_keep <= GROUP_ROWS and S % COUNT_ROWS == 0
    return pl.pallas_call(
        functools.partial(_dsa_sel_kernel, n_keep=n_keep, seq=S),
        grid=(B, S // TILE),
        in_specs=[pl.BlockSpec((1, N_IDX_HEADS, IDX_DIM, TILE), lambda b, i: (b, 0, 0, i)),
                  pl.BlockSpec((1, N_IDX_HEADS, TILE), lambda b, i: (b, 0, i)),
                  pl.BlockSpec((1, S, IDX_DIM), lambda b, i: (b, 0, 0))],
        out_specs=pl.BlockSpec((1, S, TILE), lambda b, i: (b, 0, i)),
        out_shape=jax.ShapeDtypeStruct((B, S, S), jnp.bfloat16),
        scratch_shapes=[pltpu.VMEM((S, TILE), jnp.int32),
                        pltpu.VMEM((GROUP_ROWS, TILE), F32),
                        pltpu.VMEM((1, TILE), jnp.int32)],
        compiler_params=_cparams(("arbitrary", "arbitrary")),
        name="dsa_select",
    )(qi_t, wi_t, ki)


def _gelu_tanh(x):
    return 0.5 * x * (1.0 + jnp.tanh(math.sqrt(2.0 / math.pi) * (x + 0.044715 * (x * x * x))))


def _rnn_kernel(xc_ref, yc_ref, cw_ref, cb_ref, wr_ref, br_ref, wi_ref, bi_ref, lam_ref, o_ref, xbuf, h_sc):
    L = xc_ref.shape[1]
    pad = 8

    @pl.when(pl.program_id(1) == 0)
    def _():
        xbuf[0:pad, :] = jnp.zeros((pad, D_RNN), F32)
        h_sc[...] = jnp.zeros(h_sc.shape, F32)

    x = xc_ref[0]
    xbuf[pad:pad + L, :] = x
    xconv = cb_ref[...]
    for tap in range(CONV_WIDTH):
        back = CONV_WIDTH - 1 - tap
        xconv = xconv + cw_ref[tap:tap + 1, :] * xbuf[pad - back:pad - back + L, :]
    xbuf[0:pad, :] = xbuf[L:L + pad, :]

    xb = xconv.astype(MXU_DTYPE)
    r = jax.nn.sigmoid(_dot(xb, wr_ref[...]) + br_ref[...])
    gi = jax.nn.sigmoid(_dot(xb, wi_ref[...]) + bi_ref[...])
    neg_lam = -lam_ref[...]
    softplus = jnp.maximum(neg_lam, 0.0) + jnp.log1p(jnp.exp(-jnp.abs(neg_lam)))
    log_a = -LRU_C * r * softplus
    a = jnp.exp(log_a)
    u = jnp.sqrt(-jnp.tanh(log_a) * (a * a + 1.0)) * (gi * xconv)

    row = lax.broadcasted_iota(jnp.int32, (L, D_RNN), 0)
    d = 1
    while d < L:
        keep = row >= d
        u = u + a * jnp.where(keep, pltpu.roll(u, d, 0), 0.0)
        a = a * jnp.where(keep, pltpu.roll(a, d, 0), 1.0)
        d *= 2
    hseq = u + a * h_sc[...]
    h_sc[...] = hseq[L - 1:L, :]
    o_ref[0] = (hseq * _gelu_tanh(yc_ref[0])).astype(o_ref.dtype)


def _rnn(xc, yc, conv_w, conv_b, wr_bd, b_r, wi_bd, b_i, lam):
    B, S, _ = xc.shape
    L = TILE
    tile = pl.BlockSpec((1, L, D_RNN), lambda b, s: (b, s, 0))
    vec = _resident((1, D_RNN))
    return pl.pallas_call(
        _rnn_kernel,
        grid=(B, S // L),
        in_specs=[tile, tile, _resident((CONV_WIDTH, D_RNN)), vec, _resident((D_RNN, D_RNN)), vec,
                  _resident((D_RNN, D_RNN)), vec, vec],
        out_specs=tile,
        out_shape=jax.ShapeDtypeStruct((B, S, D_RNN), MXU_DTYPE),
        scratch_shapes=[pltpu.VMEM((L + 8, D_RNN), F32), pltpu.VMEM((1, D_RNN), F32)],
        compiler_params=_cparams(("arbitrary", "arbitrary")),
        name="conv_rglru",
    )(xc, yc, conv_w, conv_b, wr_bd, b_r, wi_bd, b_i, lam)


def _post_kernel(x_ref, ya_ref, yb_ref, yc_ref, gt_ref, wpa, wpb, wpc, wo, g_mlp, w_up, w_down, g_fin, o_ref,
                 *, final):
    D = x_ref.shape[1]
    merged = jax.nn.sigmoid(gt_ref[:, 0:D]) * _dot(ya_ref[...], wpa[...])
    merged = merged + jax.nn.sigmoid(gt_ref[:, D:2 * D]) * _dot(yb_ref[...], wpb[...])
    merged = merged + jax.nn.sigmoid(gt_ref[:, 2 * D:3 * D]) * _dot(yc_ref[...], wpc[...])
    x1 = x_ref[...] + _dot(merged.astype(MXU_DTYPE), wo[...])
    xn = _rms(x1, g_mlp[...]).astype(MXU_DTYPE)
    d_ff = w_up.shape[1]
    acc = x1
    for c in range(d_ff // 1024):
        hid = jnp.maximum(_dot(xn, w_up[:, c * 1024:(c + 1) * 1024]), 0.0)
        acc = acc + _dot((hid * hid).astype(MXU_DTYPE), w_down[c * 1024:(c + 1) * 1024, :])
    if final:
        acc = _rms(acc, g_fin[...])
    o_ref[...] = acc


def _post(x2, ya, yb, yc, gates, wpa, wpb, wpc, wo, g_mlp, w_up, w_down, g_fin, final):
    T, D = x2.shape
    row = lambda w: pl.BlockSpec((TILE, w), lambda t: (t, 0))
    weights = (wpa, wpb, wpc, wo, g_mlp, w_up, w_down, g_fin)
    return pl.pallas_call(
        functools.partial(_post_kernel, final=final),
        grid=(T // TILE,),
        in_specs=[row(D), row(D_ATT), row(D_ATT), row(D_RNN), row(3 * D)] + [_resident(w.shape) for w in weights],
        out_specs=row(D),
        out_shape=jax.ShapeDtypeStruct((T, D), F32),
        compiler_params=_cparams(("arbitrary",)),
        name="merge_mlp",
    )(x2, ya, yb, yc, gates, *weights)


def _block_diag(w):
    return jax.scipy.linalg.block_diag(*[w[n] for n in range(w.shape[0])])


def kernel(x, rel_bias, norm_mix_g, w_in, conv_w, conv_b, w_r, b_r, w_i, b_i, lru_lambda, w_pa, w_pb, w_pc, w_o,
           norm_mlp_g, w_up, w_down, final_norm_g):
    B, S, D = x.shape
    depth = w_in.shape[0]
    nb = S // TILE
    assert S % Q_TILE == 0 and nb <= MAX_BLOCKS and S & (S - 1) == 0
    n_keep = min(DSA_MAX_TOPK, S // 4)
    cast = lambda a: a.astype(MXU_DTYPE)
    vec = lambda a: a.reshape(1, -1)

    bias_tiles = _bias_tiles(rel_bias)
    x2 = x.reshape(B * S, D)
    seq = lambda a: a.reshape(B, S, a.shape[-1])
    heads = lambda a: a.reshape(B, S, N_HEADS, a.shape[-1] // N_HEADS)
    heads_t = lambda a: jnp.transpose(heads(a), (0, 2, 3, 1))
    ones_rows = jnp.ones((B, N_HEADS, V_ROWS - HEAD_DIM, S), MXU_DTYPE)
    values_t = lambda a: jnp.concatenate([heads_t(a), ones_rows], axis=2)
    pad_heads = lambda a: jnp.pad(heads(a), ((0, 0), (0, 0), (0, 0), (0, HEAD_LANES - HEAD_DIM)))
    tokens = lambda yt: jnp.transpose(yt, (0, 3, 1, 2)).reshape(B * S, D_ATT)
    for l in range(depth):
        w = w_in[l]
        n_small = IDX_DIM + N_IDX_HEADS
        w_cat = cast(jnp.concatenate(
            [w[:, :N_ATT_COLS + n_small], jnp.zeros((D, SMALL_COLS - n_small), w.dtype),
             w[:, N_ATT_COLS + n_small:]], axis=1))
        (qa, ka, va, qb, kb, vb, qi, ki, wi, xc, yc, gates, kmean) = _inproj(x2, vec(norm_mix_g[l]), w_cat)

        kmean_pad = jnp.pad(kmean.reshape(B, nb, D_ATT), ((0, 0), (0, MAX_BLOCKS - nb), (0, 0)))
        qx, kx = _moba_sel(seq(qa), seq(ka), kmean_pad)
        y_a = _attention(heads_t(qx), kx, values_t(va), None, bias_tiles[:, :N_HEADS])

        mask_t = _dsa_sel(heads_t(qi), jnp.swapaxes(seq(wi), 1, 2), seq(ki), n_keep)
        y_b = _attention(jnp.transpose(pad_heads(qb), (0, 2, 3, 1)), pad_heads(kb).reshape(B, S, -1), values_t(vb),
                         mask_t, bias_tiles[:, N_HEADS:])

        y_c = _rnn(seq(xc), seq(yc), conv_w[l], vec(conv_b[l]), cast(_block_diag(w_r[l])), vec(b_r[l]),
                   cast(_block_diag(w_i[l])), vec(b_i[l]), vec(lru_lambda[l]))

        x2 = _post(x2, tokens(y_a), tokens(y_b), y_c.reshape(B * S, D_RNN), gates,
                   cast(w_pa[l]), cast(w_pb[l]), cast(w_pc[l]), cast(w_o[l]), vec(norm_mlp_g[l]),
                   cast(w_up[l]), cast(w_down[l]), vec(final_norm_g), final=(l == depth - 1))
    return x2.reshape(B, S, D)
```
